```python
import jax, jax.numpy as jnp
from jax import lax
import numpy as np

D_MODEL = 1024
BATCH = 8
SEQ = 4096
DEPTH = 1

CHUNK = 64
N_HEADS_SB = 8
HEAD_DIM_SB = 64
D_SB = N_HEADS_SB * HEAD_DIM_SB
N_GROUPS_SGU = 8
GROUP_DIM_SGU = 64
D_SGU = N_GROUPS_SGU * GROUP_DIM_SGU
SGU_CHUNK = 128
Q_BLOCK = 128
EPS = 1e-6
IN_WIDTHS = (D_SB, D_SB, D_SB, D_SB, D_SGU, D_SGU, D_SGU, D_MODEL, D_MODEL)
D_IN = 4 * D_SB + 3 * D_SGU + 2 * D_MODEL

kernel_name = 'stickbreak_sgu_gated_hybrid'


def rmsnorm(x, g):
    xf = x.astype(jnp.float32)
    y = xf * lax.rsqrt(jnp.mean(xf * xf, axis=-1, keepdims=True) + EPS)
    return (y * g.astype(jnp.float32)).astype(x.dtype)


def split_points():
    pts, acc = [], 0
    for w in IN_WIDTHS[:-1]:
        acc += w
        pts.append(acc)
    return pts


def stick_breaking_attention(q, k, v):
    b, s, h, dh = q.shape
    scale = dh ** -0.5
    qf = q.astype(jnp.float32).transpose(0, 2, 1, 3) * scale
    kf = k.astype(jnp.float32).transpose(0, 2, 1, 3)
    vf = v.astype(jnp.float32).transpose(0, 2, 1, 3)
    outs = []
    for start in range(0, s, Q_BLOCK):
        end = start + Q_BLOCK
        qb = qf[:, :, start:end]
        kb = kf[:, :, :end]
        vb = vf[:, :, :end]
        z = jnp.einsum('bhqd,bhkd->bhqk', qb, kb)
        t_idx = start + jnp.arange(Q_BLOCK)[:, None]
        s_idx = jnp.arange(end)[None, :]
        before = s_idx < t_idx
        log_keep = jnp.where(before, jax.nn.log_sigmoid(-z), 0.0)
        log_stick = lax.cumsum(log_keep, axis=3, reverse=True) - log_keep
        log_w = jax.nn.log_sigmoid(z) + log_stick
        w = jnp.where(before, jnp.exp(log_w), 0.0)
        outs.append(jnp.einsum('bhqk,bhkd->bhqd', w, vb))
    o = jnp.concatenate(outs, axis=2)
    return o.transpose(0, 2, 1, 3).astype(q.dtype)


def spatial_gating(u, v, ln_g, ln_b, w_s, b_s):
    b, s, g, c = v.shape
    vf = v.astype(jnp.float32)
    mu = jnp.mean(vf, axis=-1, keepdims=True)
    var = jnp.mean(jnp.square(vf - mu), axis=-1, keepdims=True)
    vn = (vf - mu) * lax.rsqrt(var + EPS) * ln_g.astype(jnp.float32) + ln_b.astype(jnp.float32)
    vn = vn.reshape(b, s // SGU_CHUNK, SGU_CHUNK, g, c)
    pos = jnp.arange(SGU_CHUNK)
    mask = (pos[None, :] // CHUNK) <= (pos[:, None] // CHUNK)
    w = jnp.where(mask[None], w_s.astype(jnp.float32), 0.0)
    mixed = jnp.einsum('gts,bnsgc->bntgc', w, vn) + b_s.astype(jnp.float32).T[None, None, :, :, None]
    return u * mixed.reshape(b, s, g, c).astype(u.dtype)


def setup_inputs(seed: int = 0) -> dict:
    key = jax.random.key(seed)
    ks = jax.random.split(key, 13)
    f32 = jnp.float32
    x = jax.random.normal(ks[0], (BATCH, SEQ, D_MODEL), f32)
    norm_g = 1.0 + 0.1 * jax.random.normal(ks[1], (DEPTH, D_MODEL), f32)
    w_in = jax.random.normal(ks[2], (DEPTH, D_MODEL, D_IN), f32) * D_MODEL ** -0.5
    sgu_ln_g = 1.0 + 0.1 * jax.random.normal(ks[3], (DEPTH, N_GROUPS_SGU, GROUP_DIM_SGU), f32)
    sgu_ln_b = 0.1 * jax.random.normal(ks[4], (DEPTH, N_GROUPS_SGU, GROUP_DIM_SGU), f32)
    w_spatial = jax.random.normal(ks[5], (DEPTH, N_GROUPS_SGU, SGU_CHUNK, SGU_CHUNK), f32) * SGU_CHUNK ** -0.5
    b_spatial = 1.0 + 0.1 * jax.random.normal(ks[6], (DEPTH, N_GROUPS_SGU, SGU_CHUNK), f32)
    w_up_a = jax.random.normal(ks[7], (DEPTH, D_SB, D_MODEL), f32) * D_SB ** -0.5
    w_up_b = jax.random.normal(ks[8], (DEPTH, D_SGU, D_MODEL), f32) * D_SGU ** -0.5
    w_out = jax.random.normal(ks[9], (DEPTH, D_MODEL, D_MODEL), f32) * D_MODEL ** -0.5
    final_norm_g = 1.0 + 0.1 * jax.random.normal(ks[10], (D_MODEL,), f32)
    return {'x': x, 'norm_g': norm_g, 'w_in': w_in, 'sgu_ln_g': sgu_ln_g, 'sgu_ln_b': sgu_ln_b,
            'w_spatial': w_spatial, 'b_spatial': b_spatial, 'w_up_a': w_up_a, 'w_up_b': w_up_b,
            'w_out': w_out, 'final_norm_g': final_norm_g}


def reference(x, norm_g, w_in, sgu_ln_g, sgu_ln_b, w_spatial, b_spatial, w_up_a, w_up_b, w_out, final_norm_g):
    b, s, _ = x.shape
    pts = split_points()
    for l in range(DEPTH):
        h = rmsnorm(x, norm_g[l])
        proj = jnp.einsum('bsd,de->bse', h, w_in[l])
        q, k, v, z_a, u_b, v_b, z_b, g_a, g_b = jnp.split(proj, pts, axis=-1)
        y_a = stick_breaking_attention(
            q.reshape(b, s, N_HEADS_SB, HEAD_DIM_SB),
            k.reshape(b, s, N_HEADS_SB, HEAD_DIM_SB),
            v.reshape(b, s, N_HEADS_SB, HEAD_DIM_SB)).reshape(b, s, D_SB) * jax.nn.silu(z_a)
        y_b = spatial_gating(
            jax.nn.gelu(u_b).reshape(b, s, N_GROUPS_SGU, GROUP_DIM_SGU),
            jax.nn.gelu(v_b).reshape(b, s, N_GROUPS_SGU, GROUP_DIM_SGU),
            sgu_ln_g[l], sgu_ln_b[l], w_spatial[l], b_spatial[l]).reshape(b, s, D_SGU) * jax.nn.silu(z_b)
        p_a = jnp.einsum('bse,ed->bsd', y_a, w_up_a[l])
        p_b = jnp.einsum('bse,ed->bsd', y_b, w_up_b[l])
        merged = jax.nn.sigmoid(g_a) * p_a + jax.nn.sigmoid(g_b) * p_b
        x = x + jnp.einsum('bsd,de->bse', merged, w_out[l])
    return rmsnorm(x, final_norm_g)
```

```python
import functools
import math

import jax
import jax.numpy as jnp
from jax import lax
from jax.experimental import pallas as pl
from jax.experimental.pallas import tpu as pltpu

D_MODEL = 1024
N_HEADS = 8
HEAD_DIM = 64
D_SB = N_HEADS * HEAD_DIM
N_GROUPS = 8
GROUP_DIM = 64
D_SGU = N_GROUPS * GROUP_DIM
SGU_CHUNK = 128
CHUNK = 64
EPS = 1e-6
D_QKV = 3 * D_SB
D_REST = D_SB + 3 * D_SGU + 2 * D_MODEL

LANES = 128
PAIRS = D_SB // LANES

ROWS_PROJ = 512
ROWS_TAIL = 256
BLK = 256

VMEM_LIMIT = 56 * 1024 * 1024

F32 = jnp.float32
BF16 = jnp.bfloat16


def _rms_h(x, g):
    ms = jnp.mean(x * x, axis=-1, keepdims=True)
    return (x * lax.rsqrt(ms + EPS) * g).astype(BF16)


def _split_bf16(a):
    hi = a.astype(BF16)
    lo = (a - hi.astype(F32)).astype(BF16)
    return jnp.concatenate([hi, lo], axis=-1)


def _qkv_kernel(x_ref, g_ref, w_ref, q_ref, k_ref, v_ref):
    h = _rms_h(x_ref[...], g_ref[...])
    scale = HEAD_DIM ** -0.5
    q = jnp.dot(h, w_ref[:, 0:D_SB], preferred_element_type=F32)
    q_ref[...] = (q * scale).astype(BF16)
    k = jnp.dot(h, w_ref[:, D_SB:2 * D_SB], preferred_element_type=F32)
    k_ref[...] = k.astype(BF16)
    v = jnp.dot(h, w_ref[:, 2 * D_SB:3 * D_SB], preferred_element_type=F32)
    v_ref[...] = v.astype(BF16)


def _qkv_proj(x2, norm_g, w_qkv):
    n = x2.shape[0]
    out = jax.ShapeDtypeStruct((n, D_SB), BF16)
    row_spec = pl.BlockSpec((ROWS_PROJ, D_MODEL), lambda i: (i, 0))
    out_spec = pl.BlockSpec((ROWS_PROJ, D_SB), lambda i: (i, 0))
    return pl.pallas_call(
        _qkv_kernel,
        grid=(n // ROWS_PROJ,),
        in_specs=[row_spec,
                  pl.BlockSpec((1, D_MODEL), lambda i: (0, 0)),
                  pl.BlockSpec((D_MODEL, D_QKV), lambda i: (0, 0))],
        out_specs=[out_spec, out_spec, out_spec],
        out_shape=[out, out, out],
        compiler_params=pltpu.CompilerParams(
            dimension_semantics=("arbitrary",), vmem_limit_bytes=VMEM_LIMIT),
        name="qkv_proj",
    )(x2, norm_g, w_qkv)


def _softplus(z):
    return jnp.maximum(z, 0.0) + jnp.log(1.0 + jnp.exp(-jnp.abs(z)))


def _attn_kernel(q_ref, k_ref, v_ref, tri_ref, o_ref):
    i = pl.program_id(2)
    lane = lax.broadcasted_iota(jnp.int32, (BLK, LANES), 1)
    first_head = lane < HEAD_DIM
    qp = q_ref[0]
    q_heads = (jnp.where(first_head, qp, jnp.zeros_like(qp)),
               jnp.where(first_head, jnp.zeros_like(qp), qp))
    tri = tri_ref[...]

    def block(start, q_h, carry, masked):
        kb = k_ref[0, pl.ds(start, BLK), :]
        vb = v_ref[0, pl.ds(start, BLK), :]
        z = lax.dot_general(q_h, kb, (((1,), (1,)), ((), ())), preferred_element_type=F32)
        sp = _softplus(z)
        if masked:
            row = lax.broadcasted_iota(jnp.int32, (BLK, BLK), 0)
            col = lax.broadcasted_iota(jnp.int32, (BLK, BLK), 1)
            before = col < row
            sp = jnp.where(before, sp, 0.0)
        csum = jnp.dot(_split_bf16(sp), tri, preferred_element_type=F32)
        arg = z - csum
        if carry is not None:
            arg = arg - carry
        w = jnp.exp(arg)
        if masked:
            w = jnp.where(before, w, 0.0)
        pv = jnp.dot(w.astype(BF16), vb, preferred_element_type=F32)
        return pv, csum[:, 0:1]

    diag = pl.multiple_of(i * BLK, BLK)
    accs, carries = [], []
    for q_h in q_heads:
        pv, tot = block(diag, q_h, None, True)
        accs.append(pv)
        carries.append(tot)

    def body(t, state):
        a0, a1, c0, c1 = state
        start = pl.multiple_of((i - 1 - t) * BLK, BLK)
        pv0, t0 = block(start, q_heads[0], c0, False)
        pv1, t1 = block(start, q_heads[1], c1, False)
        return a0 + pv0, a1 + pv1, c0 + t0, c1 + t1

    a0, a1, _, _ = lax.fori_loop(0, i, body, (accs[0], accs[1], carries[0], carries[1]))
    o_ref[0] = jnp.where(first_head, a0, a1)


def _attention(q, k, v, tri2):
    b, s, _ = q.shape
    blk_spec = pl.BlockSpec((1, BLK, LANES), lambda bi, p, i: (bi, i, p))
    seq_spec = pl.BlockSpec((1, s, LANES), lambda bi, p, i: (bi, 0, p))
    return pl.pallas_call(
        _attn_kernel,
        grid=(b, PAIRS, s // BLK),
        in_specs=[blk_spec, seq_spec, seq_spec,
                  pl.BlockSpec((2 * BLK, BLK), lambda bi, p, i: (0, 0))],
        out_specs=blk_spec,
        out_shape=jax.ShapeDtypeStruct((b, s, D_SB), F32),
        compiler_params=pltpu.CompilerParams(
            dimension_semantics=("arbitrary", "arbitrary", "arbitrary"),
            vmem_limit_bytes=VMEM_LIMIT),
        name="stickbreak",
    )(q, k, v, tri2)


def _tail_kernel(x_ref, o_ref, ng_ref, w_ref, lng_ref, lnb_ref, ws_ref, bs_ref, gm_ref,
                 wua_ref, wub_ref, wo_ref, fg_ref, out_ref, mixed_ref):
    x = x_ref[...]
    h = _rms_h(x, ng_ref[...])

    def proj(lo, hi):
        return jnp.dot(h, w_ref[:, lo:hi], preferred_element_type=F32)

    c_za, c_ub, c_vb, c_zb, c_ga, c_gb = 0, 512, 1024, 1536, 2048, 3072

    vg = jax.nn.gelu(proj(c_vb, c_zb))
    gm = gm_ref[...]
    mu = jnp.dot(_split_bf16(vg), gm, preferred_element_type=F32)
    d = vg - mu
    var = jnp.dot(_split_bf16(d * d), gm, preferred_element_type=F32)
    vn = (d * lax.rsqrt(var + EPS) * lng_ref[...] + lnb_ref[...]).astype(BF16)

    pos_t = lax.broadcasted_iota(jnp.int32, (2 * SGU_CHUNK, SGU_CHUNK), 0) % SGU_CHUNK
    pos_s = lax.broadcasted_iota(jnp.int32, (2 * SGU_CHUNK, SGU_CHUNK), 1)
    causal = (pos_s // CHUNK) <= (pos_t // CHUNK)
    lane = lax.broadcasted_iota(jnp.int32, (SGU_CHUNK, LANES), 1)
    first_group = lane < GROUP_DIM
    for p in range(PAIRS):
        wsp = jnp.where(causal, ws_ref[p], 0.0).astype(BF16)
        for c in range(ROWS_TAIL // SGU_CHUNK):
            rows = slice(c * SGU_CHUNK, (c + 1) * SGU_CHUNK)
            r = jnp.dot(wsp, vn[rows, p * LANES:(p + 1) * LANES], preferred_element_type=F32)
            mixed_ref[rows, p * LANES:(p + 1) * LANES] = jnp.where(
                first_group, r[0:SGU_CHUNK], r[SGU_CHUNK:2 * SGU_CHUNK])
    bias = jnp.concatenate([bs_ref[...]] * (ROWS_TAIL // SGU_CHUNK), axis=0)
    ug = jax.nn.gelu(proj(c_ub, c_vb))
    y_b = ug * (mixed_ref[...] + bias) * jax.nn.silu(proj(c_zb, c_ga))
    p_b = jnp.dot(y_b.astype(BF16), wub_ref[...], preferred_element_type=F32)
    merged = jax.nn.sigmoid(proj(c_gb, D_REST)) * p_b

    y_a = o_ref[...] * jax.nn.silu(proj(c_za, c_ub))
    p_a = jnp.dot(y_a.astype(BF16), wua_ref[...], preferred_element_type=F32)
    merged = merged + jax.nn.sigmoid(proj(c_ga, c_gb)) * p_a

    y = x + jnp.dot(merged.astype(BF16), wo_ref[...], preferred_element_type=F32)
    ms = jnp.mean(y * y, axis=-1, keepdims=True)
    out_ref[...] = y * lax.rsqrt(ms + EPS) * fg_ref[...]


def _tail(x2, o2, norm_g, w_rest, ln_g, ln_b, ws, bs, gm2, w_up_a, w_up_b, w_out, final_g):
    n = x2.shape[0]
    const = lambda shape: pl.BlockSpec(shape, lambda i: (0,) * len(shape))
    return pl.pallas_call(
        _tail_kernel,
        grid=(n // ROWS_TAIL,),
        in_specs=[pl.BlockSpec((ROWS_TAIL, D_MODEL), lambda i: (i, 0)),
                  pl.BlockSpec((ROWS_TAIL, D_SB), lambda i: (i, 0)),
                  const((1, D_MODEL)),
                  const((D_MODEL, D_REST)),
                  const((1, D_SGU)), const((1, D_SGU)),
                  const((PAIRS, 2 * SGU_CHUNK, SGU_CHUNK)),
                  const((SGU_CHUNK, D_SGU)),
                  const((2 * D_SGU, D_SGU)),
                  const((D_SB, D_MODEL)), const((D_SGU, D_MODEL)), const((D_MODEL, D_MODEL)),
                  const((1, D_MODEL))],
        out_specs=pl.BlockSpec((ROWS_TAIL, D_MODEL), lambda i: (i, 0)),
        out_shape=jax.ShapeDtypeStruct((n, D_MODEL), F32),
        scratch_shapes=[pltpu.VMEM((ROWS_TAIL, D_SGU), F32)],
        compiler_params=pltpu.CompilerParams(
            dimension_semantics=("arbitrary",), vmem_limit_bytes=VMEM_LIMIT),
        name="tail",
    )(x2, o2, norm_g, w_rest, ln_g, ln_b, ws, bs, gm2, w_up_a, w_up_b, w_out, final_g)


def _layer(x, norm_g, w_in, ln_g, ln_b, w_s, b_s, w_up_a, w_up_b, w_out, final_g):
    b, s, _ = x.shape
    x2 = x.reshape(b * s, D_MODEL)
    w_in16 = w_in.astype(BF16)
    q, k, v = _qkv_proj(x2, norm_g.reshape(1, D_MODEL), w_in16[:, :D_QKV])

    j = jnp.arange(BLK)
    tri = (j[:, None] >= j[None, :]).astype(BF16)
    tri2 = jnp.concatenate([tri, tri], axis=0)
    o = _attention(q.reshape(b, s, D_SB), k.reshape(b, s, D_SB), v.reshape(b, s, D_SB), tri2)

    gmean = jnp.kron(jnp.eye(N_GROUPS, dtype=F32),
                     jnp.full((GROUP_DIM, GROUP_DIM), 1.0 / GROUP_DIM, F32)).astype(BF16)
    gm2 = jnp.concatenate([gmean, gmean], axis=0)
    bias = jnp.repeat(b_s.T, GROUP_DIM, axis=1)
    return _tail(x2, o.reshape(b * s, D_SB), norm_g.reshape(1, D_MODEL), w_in16[:, D_QKV:],
                 ln_g.reshape(1, D_SGU), ln_b.reshape(1, D_SGU),
                 w_s.reshape(PAIRS, 2 * SGU_CHUNK, SGU_CHUNK), bias, gm2,
                 w_up_a.astype(BF16), w_up_b.astype(BF16), w_out.astype(BF16),
                 final_g.reshape(1, D_MODEL)).reshape(b, s, D_MODEL)


def kernel(x, norm_g, w_in, sgu_ln_g, sgu_ln_b, w_spatial, b_spatial, w_up_a, w_up_b, w_out,
           final_norm_g):
    assert norm_g.shape[0] == 1, "single-layer block"
    assert x.shape[1] % BLK == 0 and (x.shape[0] * x.shape[1]) % ROWS_PROJ == 0
    return _layer(x, norm_g[0], w_in[0], sgu_ln_g[0], sgu_ln_b[0], w_spatial[0], b_spatial[0],
                  w_up_a[0], w_up_b[0], w_out[0], final_norm_g)
```

```python
import functools
import math

import jax
import jax.numpy as jnp
from jax import lax
from jax.experimental import pallas as pl
from jax.experimental.pallas import tpu as pltpu

D_MODEL = 1024
N_HEADS = 8
HEAD_DIM = 64
D_SB = N_HEADS * HEAD_DIM
N_GROUPS = 8
GROUP_DIM = 64
D_SGU = N_GROUPS * GROUP_DIM
SGU_CHUNK = 128
CHUNK = 64
EPS = 1e-6
D_QKV = 3 * D_SB
D_REST = D_SB + 3 * D_SGU + 2 * D_MODEL

LANES = 128
PAIRS = D_SB // LANES

ROWS_PROJ = 512
ROWS_TAIL = 256
BLK = 256

VMEM_LIMIT = 56 * 1024 * 1024

F32 = jnp.float32
BF16 = jnp.bfloat16


def _rms_h(x, g):
    ms = jnp.mean(x * x, axis=-1, keepdims=True)
    return (x * lax.rsqrt(ms + EPS) * g).astype(BF16)


def _split_bf16(a):
    hi = a.astype(BF16)
    lo = (a - hi.astype(F32)).astype(BF16)
    return jnp.concatenate([hi, lo], axis=-1)


def _qkv_kernel(x_ref, g_ref, w_ref, q_ref, k_ref, v_ref):
    h = _rms_h(x_ref[...], g_ref[...])
    scale = HEAD_DIM ** -0.5
    for j, (ref, mul) in enumerate(((q_ref, scale), (k_ref, None), (v_ref, None))):
        y = jnp.dot(h, w_ref[:, j * D_SB:(j + 1) * D_SB], preferred_element_type=F32)
        if mul is not None:
            y = y * mul
        y = y.astype(BF16)
        for p in range(PAIRS):
            ref[p] = y[:, p * LANES:(p + 1) * LANES]


def _qkv_proj(x2, norm_g, w_qkv):
    n = x2.shape[0]
    out = jax.ShapeDtypeStruct((PAIRS, n, LANES), BF16)
    row_spec = pl.BlockSpec((ROWS_PROJ, D_MODEL), lambda i: (i, 0))
    out_spec = pl.BlockSpec((PAIRS, ROWS_PROJ, LANES), lambda i: (0, i, 0))
    return pl.pallas_call(
        _qkv_kernel,
        grid=(n // ROWS_PROJ,),
        in_specs=[row_spec,
                  pl.BlockSpec((1, D_MODEL), lambda i: (0, 0)),
                  pl.BlockSpec((D_MODEL, D_QKV), lambda i: (0, 0))],
        out_specs=[out_spec, out_spec, out_spec],
        out_shape=[out, out, out],
        compiler_params=pltpu.CompilerParams(
            dimension_semantics=("arbitrary",), vmem_limit_bytes=VMEM_LIMIT),
        name="qkv_proj",
    )(x2, norm_g, w_qkv)


def _softplus(z):
    return jnp.maximum(z, 0.0) + jnp.log(1.0 + jnp.exp(-jnp.abs(z)))


STICK_GONE = 104.0


def _attn_kernel(q_ref, k_ref, v_ref, tri_ref, o_ref, acc_ref, carry_ref):
    i = pl.program_id(1)
    lane = lax.broadcasted_iota(jnp.int32, (BLK, LANES), 1)
    first_head = lane < HEAD_DIM
    row = lax.broadcasted_iota(jnp.int32, (BLK, BLK), 0)
    col = lax.broadcasted_iota(jnp.int32, (BLK, BLK), 1)
    before = col < row

    def q_head(p, hh):
        qp = q_ref[p, 0]
        keep = first_head if hh == 0 else jnp.logical_not(first_head)
        return jnp.where(keep, qp, jnp.zeros_like(qp))

    def block(p, start, q_h, carry, masked):
        kb = k_ref[p, 0, pl.ds(start, BLK), :]
        vb = v_ref[p, 0, pl.ds(start, BLK), :]
        z = lax.dot_general(q_h, kb, (((1,), (1,)), ((), ())), preferred_element_type=F32)
        sp = _softplus(z)
        if masked:
            sp = jnp.where(before, sp, 0.0)
        csum = jnp.dot(_split_bf16(sp), tri_ref[...], preferred_element_type=F32)
        arg = z - csum
        if carry is not None:
            arg = arg - jnp.concatenate([carry] * (BLK // LANES), axis=1)
        w = jnp.exp(arg)
        if masked:
            w = jnp.where(before, w, 0.0)
        pv = jnp.dot(w.astype(BF16), vb, preferred_element_type=F32)
        return pv, jnp.broadcast_to(csum[:, 0:1], (BLK, LANES))

    diag = pl.multiple_of(i * BLK, BLK)
    prev = pl.multiple_of(jnp.maximum(i - 1, 0) * BLK, BLK)
    has_prev = i > 0
    least = None
    for p in range(PAIRS):
        for hh in range(2):
            q_h = q_head(p, hh)
            pv_d, c_d = block(p, diag, q_h, None, True)
            pv_p, c_p = block(p, prev, q_h, c_d, False)
            acc_ref[2 * p + hh] = pv_d + jnp.where(has_prev, pv_p, 0.0)
            carry = c_d + c_p
            carry_ref[2 * p + hh] = carry
            least = carry if least is None else jnp.minimum(least, carry)

    def cond(state):
        return state[1]

    def body(state):
        t, _ = state
        start = pl.multiple_of((i - 2 - t) * BLK, BLK)
        least = None
        for p in range(PAIRS):
            for hh in range(2):
                carry = carry_ref[2 * p + hh]
                pv, tot = block(p, start, q_head(p, hh), carry, False)
                acc_ref[2 * p + hh] += pv
                carry = carry + tot
                carry_ref[2 * p + hh] = carry
                least = carry if least is None else jnp.minimum(least, carry)
        return t + 1, jnp.logical_and(t + 2 < i, jnp.min(least) < STICK_GONE)

    lax.while_loop(cond, body, (jnp.int32(0), jnp.logical_and(i >= 2, jnp.min(least) < STICK_GONE)))
    for p in range(PAIRS):
        o_ref[0, :, p * LANES:(p + 1) * LANES] = jnp.where(
            first_head, acc_ref[2 * p], acc_ref[2 * p + 1])


def _attention(q, k, v, tri2):
    _, b, s, _ = q.shape
    seq_spec = pl.BlockSpec((PAIRS, 1, s, LANES), lambda bi, i: (0, bi, 0, 0))
    return pl.pallas_call(
        _attn_kernel,
        grid=(b, s // BLK),
        in_specs=[pl.BlockSpec((PAIRS, 1, BLK, LANES), lambda bi, i: (0, bi, i, 0)),
                  seq_spec, seq_spec,
                  pl.BlockSpec((2 * BLK, BLK), lambda bi, i: (0, 0))],
        out_specs=pl.BlockSpec((1, BLK, D_SB), lambda bi, i: (bi, i, 0)),
        out_shape=jax.ShapeDtypeStruct((b, s, D_SB), F32),
        scratch_shapes=[pltpu.VMEM((N_HEADS, BLK, LANES), F32),
                        pltpu.VMEM((N_HEADS, BLK, LANES), F32)],
        compiler_params=pltpu.CompilerParams(
            dimension_semantics=("arbitrary", "arbitrary"),
            vmem_limit_bytes=VMEM_LIMIT),
        name="stickbreak",
    )(q, k, v, tri2)


def _tail_kernel(x_ref, o_ref, ng_ref, w_ref, lng_ref, lnb_ref, ws_ref, bs_ref, gm_ref,
                 wua_ref, wub_ref, wo_ref, fg_ref, out_ref, mixed_ref):
    x = x_ref[...]
    h = _rms_h(x, ng_ref[...])

    def proj(lo, hi):
        return jnp.dot(h, w_ref[:, lo:hi], preferred_element_type=F32)

    c_za, c_ub, c_vb, c_zb, c_ga, c_gb = 0, 512, 1024, 1536, 2048, 3072

    vg = jax.nn.gelu(proj(c_vb, c_zb))
    gm = gm_ref[...]
    mu = jnp.dot(_split_bf16(vg), gm, preferred_element_type=F32)
    d = vg - mu
    var = jnp.dot(_split_bf16(d * d), gm, preferred_element_type=F32)
    vn = (d * lax.rsqrt(var + EPS) * lng_ref[...] + lnb_ref[...]).astype(BF16)

    pos_t = lax.broadcasted_iota(jnp.int32, (2 * SGU_CHUNK, SGU_CHUNK), 0) % SGU_CHUNK
    pos_s = lax.broadcasted_iota(jnp.int32, (2 * SGU_CHUNK, SGU_CHUNK), 1)
    causal = (pos_s // CHUNK) <= (pos_t // CHUNK)
    lane = lax.broadcasted_iota(jnp.int32, (SGU_CHUNK, LANES), 1)
    first_group = lane < GROUP_DIM
    for p in range(PAIRS):
        wsp = jnp.where(causal, ws_ref[p], 0.0).astype(BF16)
        for c in range(ROWS_TAIL // SGU_CHUNK):
            rows = slice(c * SGU_CHUNK, (c + 1) * SGU_CHUNK)
            r = jnp.dot(wsp, vn[rows, p * LANES:(p + 1) * LANES], preferred_element_type=F32)
            mixed_ref[rows, p * LANES:(p + 1) * LANES] = jnp.where(
                first_group, r[0:SGU_CHUNK], r[SGU_CHUNK:2 * SGU_CHUNK])
    bias = jnp.concatenate([bs_ref[...]] * (ROWS_TAIL // SGU_CHUNK), axis=0)
    ug = jax.nn.gelu(proj(c_ub, c_vb))
    y_b = ug * (mixed_ref[...] + bias) * jax.nn.silu(proj(c_zb, c_ga))
    p_b = jnp.dot(y_b.astype(BF16), wub_ref[...], preferred_element_type=F32)
    merged = jax.nn.sigmoid(proj(c_gb, D_REST)) * p_b

    y_a = o_ref[...] * jax.nn.silu(proj(c_za, c_ub))
    p_a = jnp.dot(y_a.astype(BF16), wua_ref[...], preferred_element_type=F32)
    merged = merged + jax.nn.sigmoid(proj(c_ga, c_gb)) * p_a

    y = x + jnp.dot(merged.astype(BF16), wo_ref[...], preferred_element_type=F32)
    ms = jnp.mean(y * y, axis=-1, keepdims=True)
    out_ref[...] = y * lax.rsqrt(ms + EPS) * fg_ref[...]


def _tail(x2, o2, norm_g, w_rest, ln_g, ln_b, ws, bs, gm2, w_up_a, w_up_b, w_out, final_g):
    n = x2.shape[0]
    const = lambda shape: pl.BlockSpec(shape, lambda i: (0,) * len(shape))
    return pl.pallas_call(
        _tail_kernel,
        grid=(n // ROWS_TAIL,),
        in_specs=[pl.BlockSpec((ROWS_TAIL, D_MODEL), lambda i: (i, 0)),
                  pl.BlockSpec((ROWS_TAIL, D_SB), lambda i: (i, 0)),
                  const((1, D_MODEL)),
                  const((D_MODEL, D_REST)),
                  const((1, D_SGU)), const((1, D_SGU)),
                  const((PAIRS, 2 * SGU_CHUNK, SGU_CHUNK)),
                  const((SGU_CHUNK, D_SGU)),
                  const((2 * D_SGU, D_SGU)),
                  const((D_SB, D_MODEL)), const((D_SGU, D_MODEL)), const((D_MODEL, D_MODEL)),
                  const((1, D_MODEL))],
        out_specs=pl.BlockSpec((ROWS_TAIL, D_MODEL), lambda i: (i, 0)),
        out_shape=jax.ShapeDtypeStruct((n, D_MODEL), F32),
        scratch_shapes=[pltpu.VMEM((ROWS_TAIL, D_SGU), F32)],
        compiler_params=pltpu.CompilerParams(
            dimension_semantics=("arbitrary",), vmem_limit_bytes=VMEM_LIMIT),
        name="tail",
    )(x2, o2, norm_g, w_rest, ln_g, ln_b, ws, bs, gm2, w_up_a, w_up_b, w_out, final_g)


def _layer(x, norm_g, w_in, ln_g, ln_b, w_s, b_s, w_up_a, w_up_b, w_out, final_g):
    b, s, _ = x.shape
    x2 = x.reshape(b * s, D_MODEL)
    w_in16 = w_in.astype(BF16)
    q, k, v = _qkv_proj(x2, norm_g.reshape(1, D_MODEL), w_in16[:, :D_QKV])

    j = jnp.arange(BLK)
    tri = (j[:, None] >= j[None, :]).astype(BF16)
    tri2 = jnp.concatenate([tri, tri], axis=0)
    pair_major = lambda a: a.reshape(PAIRS, b, s, LANES)
    o = _attention(pair_major(q), pair_major(k), pair_major(v), tri2)

    gmean = jnp.kron(jnp.eye(N_GROUPS, dtype=F32),
                     jnp.full((GROUP_DIM, GROUP_DIM), 1.0 / GROUP_DIM, F32)).astype(BF16)
    gm2 = jnp.concatenate([gmean, gmean], axis=0)
    bias = jnp.repeat(b_s.T, GROUP_DIM, axis=1)
    return _tail(x2, o.reshape(b * s, D_SB), norm_g.reshape(1, D_MODEL), w_in16[:, D_QKV:],
                 ln_g.reshape(1, D_SGU), ln_b.reshape(1, D_SGU),
                 w_s.reshape(PAIRS, 2 * SGU_CHUNK, SGU_CHUNK), bias, gm2,
                 w_up_a.astype(BF16), w_up_b.astype(BF16), w_out.astype(BF16),
                 final_g.reshape(1, D_MODEL)).reshape(b, s, D_MODEL)


def kernel(x, norm_g, w_in, sgu_ln_g, sgu_ln_b, w_spatial, b_spatial, w_up_a, w_up_b, w_out,
           final_norm_g):
    assert norm_g.shape[0] == 1, "single-layer block"
    assert x.shape[1] % BLK == 0 and (x.shape[0] * x.shape[1]) % ROWS_PROJ == 0
    return _layer(x, norm_g[0], w_in[0], sgu_ln_g[0], sgu_ln_b[0], w_spatial[0], b_spatial[0],
                  w_up_a[0], w_up_b[0], w_out[0], final_norm_g)
```

```python
import jax
import jax.numpy as jnp
from jax import lax
from jax.experimental import pallas as pl
from jax.experimental.pallas import tpu as pltpu

D_MODEL = 1024
N_HEADS = 8
HEAD_DIM = 64
D_SB = N_HEADS * HEAD_DIM
N_GROUPS = 8
GROUP_DIM = 64
D_SGU = N_GROUPS * GROUP_DIM
SGU_CHUNK = 128
CHUNK = 64
EPS = 1e-6
D_QKV = 3 * D_SB
D_REST = D_SB + 3 * D_SGU + 2 * D_MODEL

LANES = 128
PAIRS = D_SB // LANES

ROWS_PROJ = 512
ROWS_TAIL = 512
BLK = 256
QT = 128
TILES = BLK // QT
CHAINS = TILES * PAIRS

VMEM_LIMIT = 56 * 1024 * 1024

F32 = jnp.float32
BF16 = jnp.bfloat16


def _rms_h(x, g):
    ms = jnp.mean(x * x, axis=-1, keepdims=True)
    return (x * lax.rsqrt(ms + EPS) * g).astype(BF16)


def _split_bf16(a):
    hi = a.astype(BF16)
    lo = (a - hi.astype(F32)).astype(BF16)
    return jnp.concatenate([hi, lo], axis=-1)


def _split_bf16_nonneg(a):
    hi = lax.bitcast_convert_type(lax.bitcast_convert_type(a, jnp.int32) & jnp.int32(-65536), F32)
    return jnp.concatenate([hi.astype(BF16), (a - hi).astype(BF16)], axis=-1)


def _qkv_kernel(x_ref, g_ref, w_ref, q_ref, k_ref, v_ref):
    h = _rms_h(x_ref[...], g_ref[...])
    scale = HEAD_DIM ** -0.5
    for j, (ref, mul) in enumerate(((q_ref, scale), (k_ref, None), (v_ref, None))):
        y = jnp.dot(h, w_ref[:, j * D_SB:(j + 1) * D_SB], preferred_element_type=F32)
        if mul is not None:
            y = y * mul
        y = y.astype(BF16)
        for p in range(PAIRS):
            ref[p] = y[:, p * LANES:(p + 1) * LANES]


def _qkv_proj(x2, norm_g, w_qkv):
    n = x2.shape[0]
    out = jax.ShapeDtypeStruct((PAIRS, n, LANES), BF16)
    row_spec = pl.BlockSpec((ROWS_PROJ, D_MODEL), lambda i: (i, 0))
    out_spec = pl.BlockSpec((PAIRS, ROWS_PROJ, LANES), lambda i: (0, i, 0))
    return pl.pallas_call(
        _qkv_kernel,
        grid=(n // ROWS_PROJ,),
        in_specs=[row_spec,
                  pl.BlockSpec((1, D_MODEL), lambda i: (0, 0)),
                  pl.BlockSpec((D_MODEL, D_QKV), lambda i: (0, 0))],
        out_specs=[out_spec, out_spec, out_spec],
        out_shape=[out, out, out],
        compiler_params=pltpu.CompilerParams(
            dimension_semantics=("arbitrary",), vmem_limit_bytes=VMEM_LIMIT),
        name="qkv_proj",
    )(x2, norm_g, w_qkv)


STICK_GONE = 88.0
MASKED = -1e30


def _softplus(z):
    neg_abs = lax.bitcast_convert_type(
        lax.bitcast_convert_type(z, jnp.int32) | jnp.int32(-2 ** 31), F32)
    return jnp.maximum(z, 0.0) + jnp.log(1.0 + jnp.exp(neg_abs))


def _attn_kernel(q_ref, k_ref, v_ref, tri_ref, o_ref, acc_ref, carry_ref, more_ref):
    t0 = pl.program_id(1) * BLK
    lane = lax.broadcasted_iota(jnp.int32, (QT, LANES), 1)
    first_head = lane < HEAD_DIM
    row = lax.broadcasted_iota(jnp.int32, (2 * QT, BLK), 0) & (QT - 1)
    col = lax.broadcasted_iota(jnp.int32, (2 * QT, BLK), 1)

    def q_chain(p, r0):
        qt = q_ref[p, 0, pl.ds(r0, QT), :]
        zero = jnp.zeros_like(qt)
        return jnp.concatenate([jnp.where(first_head, qt, zero), jnp.where(first_head, zero, qt)],
                               axis=0)

    def scores(p, start, q2, visible):
        kb = k_ref[p, 0, pl.ds(start, BLK), :]
        z = lax.dot_general(q2, kb, (((1,), (1,)), ((), ())), preferred_element_type=F32)
        return jnp.where(visible, z, MASKED)

    def stick(z):
        return jnp.dot(_split_bf16_nonneg(_softplus(z)), tri_ref[...], preferred_element_type=F32)

    def values(p, start, z, csum, carry):
        arg = z - csum
        if carry is not None:
            arg = arg - jnp.concatenate([carry] * (BLK // LANES), axis=1)
        vb = v_ref[p, 0, pl.ds(start, BLK), :]
        pv = jnp.dot(jnp.exp(arg).astype(BF16), vb, preferred_element_type=F32)
        return pv, jnp.broadcast_to(csum[:, 0:1], (2 * QT, LANES))

    def window_start(tile):
        return jnp.maximum(t0 + (tile + 1) * QT - BLK, 0)

    chains = [(tile, p) for tile in range(TILES) for p in range(PAIRS)]
    starts = [pl.multiple_of(window_start(tile), QT) for tile in range(TILES)]
    zs = []
    for tile, p in chains:
        visible = col - row < t0 + tile * QT - starts[tile]
        zs.append(scores(p, starts[tile], q_chain(p, tile * QT), visible))
    csums = [stick(z) for z in zs]
    least = []
    for c, (tile, p) in enumerate(chains):
        pv, tot = values(p, starts[tile], zs[c], csums[c], None)
        acc_ref[c] = pv
        carry_ref[c] = tot
        least.append(jnp.min(tot, axis=0, keepdims=True))
    for c, (tile, p) in enumerate(chains):
        more_ref[c] = jnp.logical_and(starts[tile] > 0,
                                      least[c][0, 0] < STICK_GONE).astype(jnp.int32)

    def older_keys(c, _):
        @pl.when(more_ref[c] != 0)
        def _():
            tile = c // PAIRS
            p = c % PAIRS
            q2 = q_chain(p, pl.multiple_of(tile * QT, QT))

            def body(state):
                end, _ = state
                start = pl.multiple_of(jnp.maximum(end - BLK, 0), QT)
                z = scores(p, start, q2, col < end - start)
                pv, tot = values(p, start, z, stick(z), carry_ref[c])
                acc_ref[c] += pv
                carry = carry_ref[c] + tot
                carry_ref[c] = carry
                return start, jnp.logical_and(start > 0, jnp.min(carry) < STICK_GONE)

            lax.while_loop(lambda state: state[1], body, (window_start(tile), True))
        return 0

    lax.fori_loop(0, CHAINS, older_keys, 0)

    for tile in range(TILES):
        for p in range(PAIRS):
            a = acc_ref[tile * PAIRS + p]
            o_ref[0, tile * QT:(tile + 1) * QT, p * LANES:(p + 1) * LANES] = jnp.where(
                first_head, a[0:QT], a[QT:2 * QT])


def _attention(q, k, v, tri2):
    _, b, s, _ = q.shape
    seq_spec = pl.BlockSpec((PAIRS, 1, s, LANES), lambda bi, i: (0, bi, 0, 0))
    return pl.pallas_call(
        _attn_kernel,
        grid=(b, s // BLK),
        in_specs=[pl.BlockSpec((PAIRS, 1, BLK, LANES), lambda bi, i: (0, bi, i, 0)),
                  seq_spec, seq_spec,
                  pl.BlockSpec((2 * BLK, BLK), lambda bi, i: (0, 0))],
        out_specs=pl.BlockSpec((1, BLK, D_SB), lambda bi, i: (bi, i, 0)),
        out_shape=jax.ShapeDtypeStruct((b, s, D_SB), F32),
        scratch_shapes=[pltpu.VMEM((CHAINS, 2 * QT, LANES), F32),
                        pltpu.VMEM((CHAINS, 2 * QT, LANES), F32),
                        pltpu.SMEM((CHAINS,), jnp.int32)],
        compiler_params=pltpu.CompilerParams(
            dimension_semantics=("arbitrary", "arbitrary"),
            vmem_limit_bytes=VMEM_LIMIT),
        name="stickbreak",
    )(q, k, v, tri2)


def _tail_kernel(x_ref, o_ref, ng_ref, w_ref, lng_ref, lnb_ref, ws_ref, bs_ref, gm_ref,
                 wua_ref, wub_ref, wo_ref, fg_ref, out_ref, mixed_ref):
    x = x_ref[...]
    h = _rms_h(x, ng_ref[...])

    def proj(lo, hi):
        return jnp.dot(h, w_ref[:, lo:hi], preferred_element_type=F32)

    c_za, c_ub, c_vb, c_zb, c_ga, c_gb = 0, 512, 1024, 1536, 2048, 3072

    vg = jax.nn.gelu(proj(c_vb, c_zb))
    gm = gm_ref[...]
    mu = jnp.dot(_split_bf16(vg), gm, preferred_element_type=F32)
    d = vg - mu
    var = jnp.dot(_split_bf16(d * d), gm, preferred_element_type=F32)
    vn = (d * lax.rsqrt(var + EPS) * lng_ref[...] + lnb_ref[...]).astype(BF16)

    pos_t = lax.broadcasted_iota(jnp.int32, (2 * SGU_CHUNK, SGU_CHUNK), 0) % SGU_CHUNK
    pos_s = lax.broadcasted_iota(jnp.int32, (2 * SGU_CHUNK, SGU_CHUNK), 1)
    causal = (pos_s // CHUNK) <= (pos_t // CHUNK)
    lane = lax.broadcasted_iota(jnp.int32, (SGU_CHUNK, LANES), 1)
    first_group = lane < GROUP_DIM
    for p in range(PAIRS):
        wsp = jnp.where(causal, ws_ref[p], 0.0).astype(BF16)
        for c in range(ROWS_TAIL // SGU_CHUNK):
            rows = slice(c * SGU_CHUNK, (c + 1) * SGU_CHUNK)
            r = jnp.dot(wsp, vn[rows, p * LANES:(p + 1) * LANES], preferred_element_type=F32)
            mixed_ref[rows, p * LANES:(p + 1) * LANES] = jnp.where(
                first_group, r[0:SGU_CHUNK], r[SGU_CHUNK:2 * SGU_CHUNK])
    bias = jnp.concatenate([bs_ref[...]] * (ROWS_TAIL // SGU_CHUNK), axis=0)
    ug = jax.nn.gelu(proj(c_ub, c_vb))
    y_b = ug * (mixed_ref[...] + bias) * jax.nn.silu(proj(c_zb, c_ga))
    p_b = jnp.dot(y_b.astype(BF16), wub_ref[...], preferred_element_type=F32)
    merged = jax.nn.sigmoid(proj(c_gb, D_REST)) * p_b

    y_a = o_ref[...] * jax.nn.silu(proj(c_za, c_ub))
    p_a = jnp.dot(y_a.astype(BF16), wua_ref[...], preferred_element_type=F32)
    merged = merged + jax.nn.sigmoid(proj(c_ga, c_gb)) * p_a

    y = x + jnp.dot(merged.astype(BF16), wo_ref[...], preferred_element_type=F32)
    ms = jnp.mean(y * y, axis=-1, keepdims=True)
    out_ref[...] = y * lax.rsqrt(ms + EPS) * fg_ref[...]


def _tail(x2, o2, norm_g, w_rest, ln_g, ln_b, ws, bs, gm2, w_up_a, w_up_b, w_out, final_g):
    n = x2.shape[0]
    const = lambda shape: pl.BlockSpec(shape, lambda i: (0,) * len(shape))
    return pl.pallas_call(
        _tail_kernel,
        grid=(n // ROWS_TAIL,),
        in_specs=[pl.BlockSpec((ROWS_TAIL, D_MODEL), lambda i: (i, 0)),
                  pl.BlockSpec((ROWS_TAIL, D_SB), lambda i: (i, 0)),
                  const((1, D_MODEL)),
                  const((D_MODEL, D_REST)),
                  const((1, D_SGU)), const((1, D_SGU)),
                  const((PAIRS, 2 * SGU_CHUNK, SGU_CHUNK)),
                  const((SGU_CHUNK, D_SGU)),
                  const((2 * D_SGU, D_SGU)),
                  const((D_SB, D_MODEL)), const((D_SGU, D_MODEL)), const((D_MODEL, D_MODEL)),
                  const((1, D_MODEL))],
        out_specs=pl.BlockSpec((ROWS_TAIL, D_MODEL), lambda i: (i, 0)),
        out_shape=jax.ShapeDtypeStruct((n, D_MODEL), F32),
        scratch_shapes=[pltpu.VMEM((ROWS_TAIL, D_SGU), F32)],
        compiler_params=pltpu.CompilerParams(
            dimension_semantics=("arbitrary",), vmem_limit_bytes=VMEM_LIMIT),
        name="tail",
    )(x2, o2, norm_g, w_rest, ln_g, ln_b, ws, bs, gm2, w_up_a, w_up_b, w_out, final_g)


def _layer(x, norm_g, w_in, ln_g, ln_b, w_s, b_s, w_up_a, w_up_b, w_out, final_g):
    b, s, _ = x.shape
    x2 = x.reshape(b * s, D_MODEL)
    w_in16 = w_in.astype(BF16)
    q, k, v = _qkv_proj(x2, norm_g.reshape(1, D_MODEL), w_in16[:, :D_QKV])

    j = jnp.arange(BLK)
    tri = (j[:, None] >= j[None, :]).astype(BF16)
    tri2 = jnp.concatenate([tri, tri], axis=0)
    pair_major = lambda a: a.reshape(PAIRS, b, s, LANES)
    o = _attention(pair_major(q), pair_major(k), pair_major(v), tri2)

    gmean = jnp.kron(jnp.eye(N_GROUPS, dtype=F32),
                     jnp.full((GROUP_DIM, GROUP_DIM), 1.0 / GROUP_DIM, F32)).astype(BF16)
    gm2 = jnp.concatenate([gmean, gmean], axis=0)
    bias = jnp.repeat(b_s.T, GROUP_DIM, axis=1)
    return _tail(x2, o.reshape(b * s, D_SB), norm_g.reshape(1, D_MODEL), w_in16[:, D_QKV:],
                 ln_g.reshape(1, D_SGU), ln_b.reshape(1, D_SGU),
                 w_s.reshape(PAIRS, 2 * SGU_CHUNK, SGU_CHUNK), bias, gm2,
                 w_up_a.astype(BF16), w_up_b.astype(BF16), w_out.astype(BF16),
                 final_g.reshape(1, D_MODEL)).reshape(b, s, D_MODEL)


def kernel(x, norm_g, w_in, sgu_ln_g, sgu_ln_b, w_spatial, b_spatial, w_up_a, w_up_b, w_out,
           final_norm_g):
    assert norm_g.shape[0] == 1, "single-layer block"
    assert x.shape[1] % BLK == 0 and (x.shape[0] * x.shape[1]) % ROWS_PROJ == 0
    return _layer(x, norm_g[0], w_in[0], sgu_ln_g[0], sgu_ln_b[0], w_spatial[0], b_spatial[0],
                  w_up_a[0], w_up_b[0], w_out[0], final_norm_g)
```

```python
import jax
import jax.numpy as jnp
from jax import lax
from jax.experimental import pallas as pl
from jax.experimental.pallas import tpu as pltpu

D_MODEL = 1024
N_HEADS = 8
HEAD_DIM = 64
D_SB = N_HEADS * HEAD_DIM
N_GROUPS = 8
GROUP_DIM = 64
D_SGU = N_GROUPS * GROUP_DIM
SGU_CHUNK = 128
CHUNK = 64
EPS = 1e-6
D_QKV = 3 * D_SB
D_REST = D_SB + 3 * D_SGU + 2 * D_MODEL

LANES = 128
PAIRS = D_SB // LANES

ROWS_PROJ = 512
ROWS_TAIL = 512
BLK = 256
QT = 128
TILES = BLK // QT
CHAINS = TILES * PAIRS

VMEM_LIMIT = 56 * 1024 * 1024

F32 = jnp.float32
BF16 = jnp.bfloat16


def _rms_h(x, g):
    ms = jnp.mean(x * x, axis=-1, keepdims=True)
    return (x * lax.rsqrt(ms + EPS) * g).astype(BF16)


def _split_bf16(a):
    hi = a.astype(BF16)
    lo = (a - hi.astype(F32)).astype(BF16)
    return jnp.concatenate([hi, lo], axis=-1)


def _qkv_kernel(x_ref, g_ref, w_ref, q_ref, k_ref, v_ref):
    h = _rms_h(x_ref[...], g_ref[...])
    scale = HEAD_DIM ** -0.5
    for j, (ref, mul) in enumerate(((q_ref, scale), (k_ref, None), (v_ref, None))):
        y = jnp.dot(h, w_ref[:, j * D_SB:(j + 1) * D_SB], preferred_element_type=F32)
        if mul is not None:
            y = y * mul
        y = y.astype(BF16)
        for p in range(PAIRS):
            ref[p] = y[:, p * LANES:(p + 1) * LANES]


def _qkv_proj(x2, norm_g, w_qkv):
    n = x2.shape[0]
    out = jax.ShapeDtypeStruct((PAIRS, n, LANES), BF16)
    row_spec = pl.BlockSpec((ROWS_PROJ, D_MODEL), lambda i: (i, 0))
    out_spec = pl.BlockSpec((PAIRS, ROWS_PROJ, LANES), lambda i: (0, i, 0))
    return pl.pallas_call(
        _qkv_kernel,
        grid=(n // ROWS_PROJ,),
        in_specs=[row_spec,
                  pl.BlockSpec((1, D_MODEL), lambda i: (0, 0)),
                  pl.BlockSpec((D_MODEL, D_QKV), lambda i: (0, 0))],
        out_specs=[out_spec, out_spec, out_spec],
        out_shape=[out, out, out],
        compiler_params=pltpu.CompilerParams(
            dimension_semantics=("arbitrary",), vmem_limit_bytes=VMEM_LIMIT),
        name="qkv_proj",
    )(x2, norm_g, w_qkv)


STICK_GONE = 88.0
MASKED = -1e30
LOG2E = 1.4426950408889634


def _softplus(z):
    return jnp.maximum(z, 0.0) + jnp.log(1.0 + jnp.exp2(jnp.abs(z) * -LOG2E))


def _attn_kernel(q_ref, k_ref, v_ref, tri_ref, o_ref, acc_ref, carry_ref, more_ref):
    t0 = pl.program_id(1) * BLK
    lane = lax.broadcasted_iota(jnp.int32, (QT, LANES), 1)
    first_head = lane < HEAD_DIM
    row = lax.broadcasted_iota(jnp.int32, (2 * QT, BLK), 0) & (QT - 1)
    col = lax.broadcasted_iota(jnp.int32, (2 * QT, BLK), 1)

    def q_chain(p, r0):
        qt = q_ref[p, 0, pl.ds(r0, QT), :]
        zero = jnp.zeros_like(qt)
        return jnp.concatenate([jnp.where(first_head, qt, zero), jnp.where(first_head, zero, qt)],
                               axis=0)

    def scores(p, start, q2, visible):
        kb = k_ref[p, 0, pl.ds(start, BLK), :]
        z = lax.dot_general(q2, kb, (((1,), (1,)), ((), ())), preferred_element_type=F32)
        return jnp.where(visible, z, MASKED)

    def stick(z):
        return jnp.dot(_split_bf16(_softplus(z)), tri_ref[...], preferred_element_type=F32)

    def values(p, start, z, csum, carry):
        arg = z - csum
        if carry is not None:
            arg = arg - jnp.concatenate([carry] * (BLK // LANES), axis=1)
        vb = v_ref[p, 0, pl.ds(start, BLK), :]
        pv = jnp.dot(jnp.exp(arg).astype(BF16), vb, preferred_element_type=F32)
        return pv, jnp.broadcast_to(csum[:, 0:1], (2 * QT, LANES))

    def window_start(tile):
        return jnp.maximum(t0 + (tile + 1) * QT - BLK, 0)

    chains = [(tile, p) for tile in range(TILES) for p in range(PAIRS)]
    starts = [pl.multiple_of(window_start(tile), QT) for tile in range(TILES)]
    zs = []
    for tile, p in chains:
        visible = col - row < t0 + tile * QT - starts[tile]
        zs.append(scores(p, starts[tile], q_chain(p, tile * QT), visible))
    csums = [stick(z) for z in zs]
    least = []
    for c, (tile, p) in enumerate(chains):
        pv, tot = values(p, starts[tile], zs[c], csums[c], None)
        acc_ref[c] = pv
        carry_ref[c] = tot
        least.append(jnp.min(tot, axis=0, keepdims=True))
    for c, (tile, p) in enumerate(chains):
        more_ref[c] = jnp.logical_and(starts[tile] > 0,
                                      least[c][0, 0] < STICK_GONE).astype(jnp.int32)

    def older_keys(c, _):
        @pl.when(more_ref[c] != 0)
        def _():
            tile = c // PAIRS
            p = c % PAIRS
            q2 = q_chain(p, pl.multiple_of(tile * QT, QT))

            def body(state):
                end, _ = state
                start = pl.multiple_of(jnp.maximum(end - BLK, 0), QT)
                z = scores(p, start, q2, col < end - start)
                pv, tot = values(p, start, z, stick(z), carry_ref[c])
                acc_ref[c] += pv
                carry = carry_ref[c] + tot
                carry_ref[c] = carry
                return start, jnp.logical_and(start > 0, jnp.min(carry) < STICK_GONE)

            lax.while_loop(lambda state: state[1], body, (window_start(tile), True))
        return 0

    lax.fori_loop(0, CHAINS, older_keys, 0)

    for tile in range(TILES):
        for p in range(PAIRS):
            a = acc_ref[tile * PAIRS + p]
            o_ref[0, tile * QT:(tile + 1) * QT, p * LANES:(p + 1) * LANES] = jnp.where(
                first_head, a[0:QT], a[QT:2 * QT])


def _attention(q, k, v, tri2):
    _, b, s, _ = q.shape
    seq_spec = pl.BlockSpec((PAIRS, 1, s, LANES), lambda bi, i: (0, bi, 0, 0))
    return pl.pallas_call(
        _attn_kernel,
        grid=(b, s // BLK),
        in_specs=[pl.BlockSpec((PAIRS, 1, BLK, LANES), lambda bi, i: (0, bi, i, 0)),
                  seq_spec, seq_spec,
                  pl.BlockSpec((2 * BLK, BLK), lambda bi, i: (0, 0))],
        out_specs=pl.BlockSpec((1, BLK, D_SB), lambda bi, i: (bi, i, 0)),
        out_shape=jax.ShapeDtypeStruct((b, s, D_SB), F32),
        scratch_shapes=[pltpu.VMEM((CHAINS, 2 * QT, LANES), F32),
                        pltpu.VMEM((CHAINS, 2 * QT, LANES), F32),
                        pltpu.SMEM((CHAINS,), jnp.int32)],
        compiler_params=pltpu.CompilerParams(
            dimension_semantics=("arbitrary", "arbitrary"),
            vmem_limit_bytes=VMEM_LIMIT),
        name="stickbreak",
    )(q, k, v, tri2)


def _tail_kernel(x_ref, o_ref, ng_ref, w_ref, lng_ref, lnb_ref, ws_ref, bs_ref, gm_ref,
                 wua_ref, wub_ref, wo_ref, fg_ref, out_ref, mixed_ref):
    x = x_ref[...]
    h = _rms_h(x, ng_ref[...])

    def proj(lo, hi):
        return jnp.dot(h, w_ref[:, lo:hi], preferred_element_type=F32)

    c_za, c_ub, c_vb, c_zb, c_ga, c_gb = 0, 512, 1024, 1536, 2048, 3072

    vg = jax.nn.gelu(proj(c_vb, c_zb))
    mu = jnp.dot(_split_bf16(vg), gm_ref[...], preferred_element_type=F32)
    d = vg - mu
    var = jnp.dot((d * d).astype(BF16), gm_ref[0:D_SGU, :], preferred_element_type=F32)
    vn = (d * lax.rsqrt(var + EPS) * lng_ref[...] + lnb_ref[...]).astype(BF16)

    pos_t = lax.broadcasted_iota(jnp.int32, (2 * SGU_CHUNK, SGU_CHUNK), 0) % SGU_CHUNK
    pos_s = lax.broadcasted_iota(jnp.int32, (2 * SGU_CHUNK, SGU_CHUNK), 1)
    causal = (pos_s // CHUNK) <= (pos_t // CHUNK)
    lane = lax.broadcasted_iota(jnp.int32, (SGU_CHUNK, LANES), 1)
    first_group = lane < GROUP_DIM
    for p in range(PAIRS):
        wsp = jnp.where(causal, ws_ref[p], 0.0).astype(BF16)
        lanes = slice(p * LANES, (p + 1) * LANES)
        for c in range(0, ROWS_TAIL // SGU_CHUNK, 2):
            rows = [slice((c + j) * SGU_CHUNK, (c + j + 1) * SGU_CHUNK) for j in range(2)]
            r = jnp.dot(wsp, jnp.concatenate([vn[rows[0], lanes], vn[rows[1], lanes]], axis=1),
                        preferred_element_type=F32)
            for j in range(2):
                rj = r[:, j * LANES:(j + 1) * LANES]
                mixed_ref[rows[j], lanes] = jnp.where(
                    first_group, rj[0:SGU_CHUNK], rj[SGU_CHUNK:2 * SGU_CHUNK])
    bias = jnp.concatenate([bs_ref[...]] * (ROWS_TAIL // SGU_CHUNK), axis=0)
    ug = jax.nn.gelu(proj(c_ub, c_vb))
    y_b = ug * (mixed_ref[...] + bias) * jax.nn.silu(proj(c_zb, c_ga))
    p_b = jnp.dot(y_b.astype(BF16), wub_ref[...], preferred_element_type=F32)
    merged = jax.nn.sigmoid(proj(c_gb, D_REST)) * p_b

    y_a = o_ref[...] * jax.nn.silu(proj(c_za, c_ub))
    p_a = jnp.dot(y_a.astype(BF16), wua_ref[...], preferred_element_type=F32)
    merged = merged + jax.nn.sigmoid(proj(c_ga, c_gb)) * p_a

    y = x + jnp.dot(merged.astype(BF16), wo_ref[...], preferred_element_type=F32)
    ms = jnp.mean(y * y, axis=-1, keepdims=True)
    out_ref[...] = y * lax.rsqrt(ms + EPS) * fg_ref[...]


def _tail(x2, o2, norm_g, w_rest, ln_g, ln_b, ws, bs, gm2, w_up_a, w_up_b, w_out, final_g):
    n = x2.shape[0]
    const = lambda shape: pl.BlockSpec(shape, lambda i: (0,) * len(shape))
    return pl.pallas_call(
        _tail_kernel,
        grid=(n // ROWS_TAIL,),
        in_specs=[pl.BlockSpec((ROWS_TAIL, D_MODEL), lambda i: (i, 0)),
                  pl.BlockSpec((ROWS_TAIL, D_SB), lambda i: (i, 0)),
                  const((1, D_MODEL)),
                  pl.BlockSpec((pl.Element(D_MODEL), pl.Element(D_REST)), lambda i: (0, D_QKV)),
                  const((1, D_SGU)), const((1, D_SGU)),
                  const((PAIRS, 2 * SGU_CHUNK, SGU_CHUNK)),
                  const((SGU_CHUNK, D_SGU)),
                  const((2 * D_SGU, D_SGU)),
                  const((D_SB, D_MODEL)), const((D_SGU, D_MODEL)), const((D_MODEL, D_MODEL)),
                  const((1, D_MODEL))],
        out_specs=pl.BlockSpec((ROWS_TAIL, D_MODEL), lambda i: (i, 0)),
        out_shape=jax.ShapeDtypeStruct((n, D_MODEL), F32),
        scratch_shapes=[pltpu.VMEM((ROWS_TAIL, D_SGU), F32)],
        compiler_params=pltpu.CompilerParams(
            dimension_semantics=("arbitrary",), vmem_limit_bytes=VMEM_LIMIT),
        name="tail",
    )(x2, o2, norm_g, w_rest, ln_g, ln_b, ws, bs, gm2, w_up_a, w_up_b, w_out, final_g)


def _layer(x, norm_g, w_in, ln_g, ln_b, w_s, b_s, w_up_a, w_up_b, w_out, final_g):
    b, s, _ = x.shape
    x2 = x.reshape(b * s, D_MODEL)
    w_in16 = w_in.astype(BF16)
    q, k, v = _qkv_proj(x2, norm_g.reshape(1, D_MODEL), w_in16)

    j = jnp.arange(BLK)
    tri = (j[:, None] >= j[None, :]).astype(BF16)
    tri2 = jnp.concatenate([tri, tri], axis=0)
    pair_major = lambda a: a.reshape(PAIRS, b, s, LANES)
    o = _attention(pair_major(q), pair_major(k), pair_major(v), tri2)

    gmean = jnp.kron(jnp.eye(N_GROUPS, dtype=F32),
                     jnp.full((GROUP_DIM, GROUP_DIM), 1.0 / GROUP_DIM, F32)).astype(BF16)
    gm2 = jnp.concatenate([gmean, gmean], axis=0)
    bias = jnp.repeat(b_s.T, GROUP_DIM, axis=1)
    return _tail(x2, o.reshape(b * s, D_SB), norm_g.reshape(1, D_MODEL), w_in16,
                 ln_g.reshape(1, D_SGU), ln_b.reshape(1, D_SGU),
                 w_s.reshape(PAIRS, 2 * SGU_CHUNK, SGU_CHUNK), bias, gm2,
                 w_up_a.astype(BF16), w_up_b.astype(BF16), w_out.astype(BF16),
                 final_g.reshape(1, D_MODEL)).reshape(b, s, D_MODEL)


def kernel(x, norm_g, w_in, sgu_ln_g, sgu_ln_b, w_spatial, b_spatial, w_up_a, w_up_b, w_out,
           final_norm_g):
    assert norm_g.shape[0] == 1, "single-layer block"
    assert x.shape[1] % BLK == 0 and (x.shape[0] * x.shape[1]) % ROWS_PROJ == 0
    return _layer(x, norm_g[0], w_in[0], sgu_ln_g[0], sgu_ln_b[0], w_spatial[0], b_spatial[0],
                  w_up_a[0], w_up_b[0], w_out[0], final_norm_g)
```

```python
import jax
import jax.numpy as jnp
from jax import lax
from jax.experimental import pallas as pl
from jax.experimental.pallas import tpu as pltpu

D_MODEL = 1024
N_HEADS = 8
HEAD_DIM = 64
D_SB = N_HEADS * HEAD_DIM
N_GROUPS = 8
GROUP_DIM = 64
D_SGU = N_GROUPS * GROUP_DIM
SGU_CHUNK = 128
CHUNK = 64
EPS = 1e-6
D_QKV = 3 * D_SB
D_REST = D_SB + 3 * D_SGU + 2 * D_MODEL

LANES = 128
PAIRS = D_SB // LANES

ROWS_PROJ = 1024
ROWS_TAIL = 1024
SUB_TAIL = 512
BLK = 256
QROWS = 512
QT = 128
TILES = QROWS // QT
CHAINS = TILES * PAIRS

VMEM_LIMIT = 56 * 1024 * 1024

F32 = jnp.float32
BF16 = jnp.bfloat16


def _rms_h(x, g):
    ms = jnp.mean(x * x, axis=-1, keepdims=True)
    return (x * lax.rsqrt(ms + EPS) * g).astype(BF16)


def _split_bf16(a):
    hi = a.astype(BF16)
    lo = (a - hi.astype(F32)).astype(BF16)
    return jnp.concatenate([hi, lo], axis=-1)


def _qkv_kernel(x_ref, g_ref, w_ref, q_ref, k_ref, v_ref):
    h = _rms_h(x_ref[...], g_ref[...])
    scale = HEAD_DIM ** -0.5
    for j, (ref, mul) in enumerate(((q_ref, scale), (k_ref, None), (v_ref, None))):
        y = jnp.dot(h, w_ref[:, j * D_SB:(j + 1) * D_SB], preferred_element_type=F32)
        if mul is not None:
            y = y * mul
        y = y.astype(BF16)
        for p in range(PAIRS):
            ref[p] = y[:, p * LANES:(p + 1) * LANES]


def _qkv_proj(x2, norm_g, w_qkv):
    n = x2.shape[0]
    out = jax.ShapeDtypeStruct((PAIRS, n, LANES), BF16)
    row_spec = pl.BlockSpec((ROWS_PROJ, D_MODEL), lambda i: (i, 0))
    out_spec = pl.BlockSpec((PAIRS, ROWS_PROJ, LANES), lambda i: (0, i, 0))
    return pl.pallas_call(
        _qkv_kernel,
        grid=(n // ROWS_PROJ,),
        in_specs=[row_spec,
                  pl.BlockSpec((1, D_MODEL), lambda i: (0, 0)),
                  pl.BlockSpec((D_MODEL, D_QKV), lambda i: (0, 0))],
        out_specs=[out_spec, out_spec, out_spec],
        out_shape=[out, out, out],
        compiler_params=pltpu.CompilerParams(
            dimension_semantics=("arbitrary",), vmem_limit_bytes=VMEM_LIMIT),
        name="qkv_proj",
    )(x2, norm_g, w_qkv)


STICK_GONE = 88.0
MASKED = -1e30
LOG2E = 1.4426950408889634


def _softplus(z):
    return jnp.maximum(z, 0.0) + jnp.log(1.0 + jnp.exp2(jnp.abs(z) * -LOG2E))


def _attn_kernel(q_ref, k_ref, v_ref, tri_ref, o_ref, acc_ref, carry_ref, more_ref):
    t0 = pl.program_id(1) * QROWS
    lane = lax.broadcasted_iota(jnp.int32, (QT, LANES), 1)
    first_head = lane < HEAD_DIM
    row = lax.broadcasted_iota(jnp.int32, (2 * QT, BLK), 0) & (QT - 1)
    col = lax.broadcasted_iota(jnp.int32, (2 * QT, BLK), 1)

    def q_chain(p, r0):
        qt = q_ref[p, 0, pl.ds(r0, QT), :]
        zero = jnp.zeros_like(qt)
        return jnp.concatenate([jnp.where(first_head, qt, zero), jnp.where(first_head, zero, qt)],
                               axis=0)

    def scores(p, start, q2, visible):
        kb = k_ref[p, 0, pl.ds(start, BLK), :]
        z = lax.dot_general(q2, kb, (((1,), (1,)), ((), ())), preferred_element_type=F32)
        return jnp.where(visible, z, MASKED)

    def stick(z):
        return jnp.dot(_split_bf16(_softplus(z)), tri_ref[...], preferred_element_type=F32)

    def values(p, start, z, csum, carry):
        arg = z - csum
        if carry is not None:
            arg = arg - jnp.concatenate([carry] * (BLK // LANES), axis=1)
        vb = v_ref[p, 0, pl.ds(start, BLK), :]
        pv = jnp.dot(jnp.exp(arg).astype(BF16), vb, preferred_element_type=F32)
        return pv, jnp.broadcast_to(csum[:, 0:1], (2 * QT, LANES))

    def window_start(tile):
        return jnp.maximum(t0 + (tile + 1) * QT - BLK, 0)

    chains = [(tile, p) for tile in range(TILES) for p in range(PAIRS)]
    starts = [pl.multiple_of(window_start(tile), QT) for tile in range(TILES)]
    zs = []
    for tile, p in chains:
        visible = col - row < t0 + tile * QT - starts[tile]
        zs.append(scores(p, starts[tile], q_chain(p, tile * QT), visible))
    csums = [stick(z) for z in zs]
    least = []
    for c, (tile, p) in enumerate(chains):
        pv, tot = values(p, starts[tile], zs[c], csums[c], None)
        acc_ref[c] = pv
        carry_ref[c] = tot
        least.append(jnp.min(tot, axis=0, keepdims=True))
    for c, (tile, p) in enumerate(chains):
        more_ref[c] = jnp.logical_and(starts[tile] > 0,
                                      least[c][0, 0] < STICK_GONE).astype(jnp.int32)

    def older_keys(c, _):
        @pl.when(more_ref[c] != 0)
        def _():
            tile = c // PAIRS
            p = c % PAIRS
            q2 = q_chain(p, pl.multiple_of(tile * QT, QT))

            def body(state):
                end, _ = state
                start = pl.multiple_of(jnp.maximum(end - BLK, 0), QT)
                z = scores(p, start, q2, col < end - start)
                pv, tot = values(p, start, z, stick(z), carry_ref[c])
                acc_ref[c] += pv
                carry = carry_ref[c] + tot
                carry_ref[c] = carry
                return start, jnp.logical_and(start > 0, jnp.min(carry) < STICK_GONE)

            lax.while_loop(lambda state: state[1], body, (window_start(tile), True))
        return 0

    lax.fori_loop(0, CHAINS, older_keys, 0)

    for tile in range(TILES):
        for p in range(PAIRS):
            a = acc_ref[tile * PAIRS + p]
            o_ref[0, tile * QT:(tile + 1) * QT, p * LANES:(p + 1) * LANES] = jnp.where(
                first_head, a[0:QT], a[QT:2 * QT])


def _attention(q, k, v, tri2):
    _, b, s, _ = q.shape
    seq_spec = pl.BlockSpec((PAIRS, 1, s, LANES), lambda bi, i: (0, bi, 0, 0))
    return pl.pallas_call(
        _attn_kernel,
        grid=(b, s // QROWS),
        in_specs=[pl.BlockSpec((PAIRS, 1, QROWS, LANES), lambda bi, i: (0, bi, i, 0)),
                  seq_spec, seq_spec,
                  pl.BlockSpec((2 * BLK, BLK), lambda bi, i: (0, 0))],
        out_specs=pl.BlockSpec((1, QROWS, D_SB), lambda bi, i: (bi, i, 0)),
        out_shape=jax.ShapeDtypeStruct((b, s, D_SB), F32),
        scratch_shapes=[pltpu.VMEM((CHAINS, 2 * QT, LANES), F32),
                        pltpu.VMEM((CHAINS, 2 * QT, LANES), F32),
                        pltpu.SMEM((CHAINS,), jnp.int32)],
        compiler_params=pltpu.CompilerParams(
            dimension_semantics=("arbitrary", "arbitrary"),
            vmem_limit_bytes=VMEM_LIMIT),
        name="stickbreak",
    )(q, k, v, tri2)


def _tail_kernel(x_ref, o_ref, ng_ref, w_ref, lng_ref, lnb_ref, ws_ref, bs_ref, gm_ref,
                 wua_ref, wub_ref, wo_ref, fg_ref, out_ref, mixed_ref):
    tiles = [slice(r0, r0 + SUB_TAIL) for r0 in range(0, ROWS_TAIL, SUB_TAIL)]
    c_za, c_ub, c_vb, c_zb, c_ga, c_gb = 0, 512, 1024, 1536, 2048, 3072
    xs = [x_ref[t, :] for t in tiles]
    hs = [_rms_h(x, ng_ref[...]) for x in xs]

    def proj(h, lo, hi):
        return jnp.dot(h, w_ref[:, lo:hi], preferred_element_type=F32)

    half = lax.broadcasted_iota(jnp.int32, (SUB_TAIL, LANES), 1) // GROUP_DIM

    def spread(stats):
        return jnp.concatenate(
            [jnp.take_along_axis(stats, 2 * p + half, axis=1) for p in range(PAIRS)], axis=1)

    vgs = [jax.nn.gelu(proj(h, c_vb, c_zb)) for h in hs]
    mus = [spread(jnp.dot(_split_bf16(vg), gm_ref[...], preferred_element_type=F32)) for vg in vgs]
    ugs = [jax.nn.gelu(proj(h, c_ub, c_vb)) for h in hs]
    ds = [vg - mu for vg, mu in zip(vgs, mus)]
    vars_ = [spread(jnp.dot((d * d).astype(BF16), gm_ref[0:D_SGU, :], preferred_element_type=F32))
             for d in ds]
    sz_bs = [jax.nn.silu(proj(h, c_zb, c_ga)) for h in hs]
    vns = [(d * lax.rsqrt(var + EPS) * lng_ref[...] + lnb_ref[...]).astype(BF16)
           for d, var in zip(ds, vars_)]
    sz_as = [jax.nn.silu(proj(h, c_za, c_ub)) for h in hs]

    pos_t = lax.broadcasted_iota(jnp.int32, (2 * SGU_CHUNK, SGU_CHUNK), 0) % SGU_CHUNK
    pos_s = lax.broadcasted_iota(jnp.int32, (2 * SGU_CHUNK, SGU_CHUNK), 1)
    causal = (pos_s // CHUNK) <= (pos_t // CHUNK)
    lane = lax.broadcasted_iota(jnp.int32, (SGU_CHUNK, LANES), 1)
    first_group = lane < GROUP_DIM
    for p in range(PAIRS):
        wsp = jnp.where(causal, ws_ref[p], 0.0).astype(BF16)
        lanes = slice(p * LANES, (p + 1) * LANES)
        for t, vn in zip(tiles, vns):
            for c in range(0, SUB_TAIL, 2 * SGU_CHUNK):
                rows = [slice(c + j * SGU_CHUNK, c + (j + 1) * SGU_CHUNK) for j in range(2)]
                r = jnp.dot(wsp, jnp.concatenate([vn[rows[0], lanes], vn[rows[1], lanes]], axis=1),
                            preferred_element_type=F32)
                for j in range(2):
                    rj = r[:, j * LANES:(j + 1) * LANES]
                    mixed_ref[t.start + rows[j].start:t.start + rows[j].stop, lanes] = jnp.where(
                        first_group, rj[0:SGU_CHUNK], rj[SGU_CHUNK:2 * SGU_CHUNK])

    p_as = [jnp.dot((o_ref[t, :] * sz_a).astype(BF16), wua_ref[...], preferred_element_type=F32)
            for t, sz_a in zip(tiles, sz_as)]
    merged = [jax.nn.sigmoid(proj(h, c_ga, c_gb)) * p_a for h, p_a in zip(hs, p_as)]

    bias = jnp.concatenate([bs_ref[...]] * (SUB_TAIL // SGU_CHUNK), axis=0)
    p_bs = [jnp.dot((ug * (mixed_ref[t, :] + bias) * sz_b).astype(BF16), wub_ref[...],
                    preferred_element_type=F32)
            for t, ug, sz_b in zip(tiles, ugs, sz_bs)]
    merged = [m + jax.nn.sigmoid(proj(h, c_gb, D_REST)) * p_b
              for m, h, p_b in zip(merged, hs, p_bs)]

    ys = [x + jnp.dot(m.astype(BF16), wo_ref[...], preferred_element_type=F32)
          for x, m in zip(xs, merged)]
    for t, y in zip(tiles, ys):
        ms = jnp.mean(y * y, axis=-1, keepdims=True)
        out_ref[t, :] = y * lax.rsqrt(ms + EPS) * fg_ref[...]


def _tail(x2, o2, norm_g, w_rest, ln_g, ln_b, ws, bs, gm2, w_up_a, w_up_b, w_out, final_g):
    n = x2.shape[0]
    const = lambda shape: pl.BlockSpec(shape, lambda i: (0,) * len(shape),
                                       pipeline_mode=pl.Buffered(1))
    return pl.pallas_call(
        _tail_kernel,
        grid=(n // ROWS_TAIL,),
        in_specs=[pl.BlockSpec((ROWS_TAIL, D_MODEL), lambda i: (i, 0)),
                  pl.BlockSpec((ROWS_TAIL, D_SB), lambda i: (i, 0)),
                  const((1, D_MODEL)),
                  pl.BlockSpec((pl.Element(D_MODEL), pl.Element(D_REST)), lambda i: (0, D_QKV),
                               pipeline_mode=pl.Buffered(1)),
                  const((1, D_SGU)), const((1, D_SGU)),
                  const((PAIRS, 2 * SGU_CHUNK, SGU_CHUNK)),
                  const((SGU_CHUNK, D_SGU)),
                  const((2 * D_SGU, LANES)),
                  const((D_SB, D_MODEL)), const((D_SGU, D_MODEL)), const((D_MODEL, D_MODEL)),
                  const((1, D_MODEL))],
        out_specs=pl.BlockSpec((ROWS_TAIL, D_MODEL), lambda i: (i, 0)),
        out_shape=jax.ShapeDtypeStruct((n, D_MODEL), F32),
        scratch_shapes=[pltpu.VMEM((ROWS_TAIL, D_SGU), F32)],
        compiler_params=pltpu.CompilerParams(
            dimension_semantics=("arbitrary",), vmem_limit_bytes=VMEM_LIMIT),
        name="tail",
    )(x2, o2, norm_g, w_rest, ln_g, ln_b, ws, bs, gm2, w_up_a, w_up_b, w_out, final_g)


def _layer(x, norm_g, w_in, ln_g, ln_b, w_s, b_s, w_up_a, w_up_b, w_out, final_g):
    b, s, _ = x.shape
    x2 = x.reshape(b * s, D_MODEL)
    w_in16 = w_in.astype(BF16)
    q, k, v = _qkv_proj(x2, norm_g.reshape(1, D_MODEL), w_in16)

    j = jnp.arange(BLK)
    tri = (j[:, None] >= j[None, :]).astype(BF16)
    tri2 = jnp.concatenate([tri, tri], axis=0)
    pair_major = lambda a: a.reshape(PAIRS, b, s, LANES)
    o = _attention(pair_major(q), pair_major(k), pair_major(v), tri2)

    gmean = (jnp.arange(D_SGU)[:, None] // GROUP_DIM == jnp.arange(LANES)[None, :] % N_GROUPS)
    gmean = (gmean.astype(F32) / GROUP_DIM).astype(BF16)
    gm2 = jnp.concatenate([gmean, gmean], axis=0)
    bias = jnp.repeat(b_s.T, GROUP_DIM, axis=1)
    return _tail(x2, o.reshape(b * s, D_SB), norm_g.reshape(1, D_MODEL), w_in16,
                 ln_g.reshape(1, D_SGU), ln_b.reshape(1, D_SGU),
                 w_s.reshape(PAIRS, 2 * SGU_CHUNK, SGU_CHUNK), bias, gm2,
                 w_up_a.astype(BF16), w_up_b.astype(BF16), w_out.astype(BF16),
                 final_g.reshape(1, D_MODEL)).reshape(b, s, D_MODEL)


def kernel(x, norm_g, w_in, sgu_ln_g, sgu_ln_b, w_spatial, b_spatial, w_up_a, w_up_b, w_out,
           final_norm_g):
    assert norm_g.shape[0] == 1, "single-layer block"
    assert x.shape[1] % QROWS == 0 and (x.shape[0] * x.shape[1]) % ROWS_PROJ == 0
    return _layer(x, norm_g[0], w_in[0], sgu_ln_g[0], sgu_ln_b[0], w_spatial[0], b_spatial[0],
                  w_up_a[0], w_up_b[0], w_out[0], final_norm_g)
```

```python
import jax
import jax.numpy as jnp
from jax import lax
from jax.experimental import pallas as pl
from jax.experimental.pallas import tpu as pltpu

D_MODEL = 1024
N_HEADS = 8
HEAD_DIM = 64
D_SB = N_HEADS * HEAD_DIM
N_GROUPS = 8
GROUP_DIM = 64
D_SGU = N_GROUPS * GROUP_DIM
SGU_CHUNK = 128
CHUNK = 64
EPS = 1e-6
D_QKV = 3 * D_SB
D_REST = D_SB + 3 * D_SGU + 2 * D_MODEL

LANES = 128
PAIRS = D_SB // LANES

ROWS_PROJ = 1024
BLK = 256
ROWS = 512
QT = 128
TILES = ROWS // QT
CHAINS = TILES * PAIRS
GROUP = 4
DENSE_PER_SLOT = 2

VMEM_LIMIT = 58 * 1024 * 1024

F32 = jnp.float32
BF16 = jnp.bfloat16


def _rms_h(x, g):
    ms = jnp.mean(x * x, axis=-1, keepdims=True)
    return (x * lax.rsqrt(ms + EPS) * g).astype(BF16)


def _split_bf16(a):
    hi = a.astype(BF16)
    lo = (a - hi.astype(F32)).astype(BF16)
    return jnp.concatenate([hi, lo], axis=-1)


def _qkv_kernel(x_ref, g_ref, w_ref, q_ref, k_ref, v_ref):
    h = _rms_h(x_ref[...], g_ref[...])
    scale = HEAD_DIM ** -0.5
    for j, (ref, mul) in enumerate(((q_ref, scale), (k_ref, None), (v_ref, None))):
        y = jnp.dot(h, w_ref[:, j * D_SB:(j + 1) * D_SB], preferred_element_type=F32)
        if mul is not None:
            y = y * mul
        y = y.astype(BF16)
        for p in range(PAIRS):
            ref[p] = y[:, p * LANES:(p + 1) * LANES]


def _qkv_proj(x2, norm_g, w_qkv):
    n = x2.shape[0]
    out = jax.ShapeDtypeStruct((PAIRS, n, LANES), BF16)
    row_spec = pl.BlockSpec((ROWS_PROJ, D_MODEL), lambda i: (i, 0))
    out_spec = pl.BlockSpec((PAIRS, ROWS_PROJ, LANES), lambda i: (0, i, 0))
    return pl.pallas_call(
        _qkv_kernel,
        grid=(n // ROWS_PROJ,),
        in_specs=[row_spec,
                  pl.BlockSpec((1, D_MODEL), lambda i: (0, 0)),
                  pl.BlockSpec((D_MODEL, D_QKV), lambda i: (0, 0))],
        out_specs=[out_spec, out_spec, out_spec],
        out_shape=[out, out, out],
        compiler_params=pltpu.CompilerParams(
            dimension_semantics=("arbitrary",), vmem_limit_bytes=VMEM_LIMIT),
        name="qkv_proj",
    )(x2, norm_g, w_qkv)


STICK_GONE = 88.0
MASKED = -1e30
LOG2E = 1.4426950408889634


def _softplus(z):
    return jnp.maximum(z, 0.0) + jnp.log(1.0 + jnp.exp2(jnp.abs(z) * -LOG2E))


def _block_kernel(x_ref, q_ref, k_ref, v_ref, tri_ref, ng_ref, w_ref, lng_ref, lnb_ref, ws_ref,
                  bs_ref, gm_ref, wua_ref, wub_ref, wo_ref, fg_ref, out_ref,
                  acc_ref, carry_ref, mixed_ref, more_ref):
    t0 = pl.program_id(1) * ROWS
    lane = lax.broadcasted_iota(jnp.int32, (QT, LANES), 1)
    first_half = lane < HEAD_DIM
    row = lax.broadcasted_iota(jnp.int32, (2 * QT, BLK), 0) & (QT - 1)
    col = lax.broadcasted_iota(jnp.int32, (2 * QT, BLK), 1)

    def q_chain(p, r0):
        qt = q_ref[p, 0, pl.ds(r0, QT), :]
        zero = jnp.zeros_like(qt)
        return jnp.concatenate([jnp.where(first_half, qt, zero), jnp.where(first_half, zero, qt)],
                               axis=0)

    def scores(p, start, q2, visible):
        kb = k_ref[p, 0, pl.ds(start, BLK), :]
        z = lax.dot_general(q2, kb, (((1,), (1,)), ((), ())), preferred_element_type=F32)
        return jnp.where(visible, z, MASKED)

    def stick(z):
        return jnp.dot(_split_bf16(_softplus(z)), tri_ref[...], preferred_element_type=F32)

    def values(p, start, z, csum, carry):
        arg = z - csum
        if carry is not None:
            arg = arg - jnp.concatenate([carry] * (BLK // LANES), axis=1)
        vb = v_ref[p, 0, pl.ds(start, BLK), :]
        pv = jnp.dot(jnp.exp(arg).astype(BF16), vb, preferred_element_type=F32)
        return pv, jnp.broadcast_to(csum[:, 0:1], (2 * QT, LANES))

    def window_start(tile):
        return jnp.maximum(t0 + (tile + 1) * QT - BLK, 0)

    c_za, c_ub, c_vb, c_zb, c_ga, c_gb = 0, 512, 1024, 1536, 2048, 3072
    x = x_ref[0]
    h = _rms_h(x, ng_ref[...])

    def proj(lo, hi):
        return jnp.dot(h, w_ref[:, lo:hi], preferred_element_type=F32)

    half = lax.broadcasted_iota(jnp.int32, (ROWS, LANES), 1) // GROUP_DIM

    def spread(stats):
        return jnp.concatenate(
            [jnp.take_along_axis(stats, 2 * p + half, axis=1) for p in range(PAIRS)], axis=1)

    env = {}

    def step_vg():
        env["vg"] = jax.nn.gelu(proj(c_vb, c_zb))

    def step_mu():
        mu = spread(jnp.dot(_split_bf16(env["vg"]), gm_ref[...], preferred_element_type=F32))
        env["d"] = env["vg"] - mu

    def step_ug():
        env["ug"] = jax.nn.gelu(proj(c_ub, c_vb))

    def step_var():
        d = env["d"]
        var = spread(jnp.dot((d * d).astype(BF16), gm_ref[0:D_SGU, :], preferred_element_type=F32))
        env["vn"] = (d * lax.rsqrt(var + EPS) * lng_ref[...] + lnb_ref[...]).astype(BF16)

    def step_szb():
        env["sz_b"] = jax.nn.silu(proj(c_zb, c_ga))

    def step_mix():
        pos_t = lax.broadcasted_iota(jnp.int32, (2 * SGU_CHUNK, SGU_CHUNK), 0) % SGU_CHUNK
        pos_s = lax.broadcasted_iota(jnp.int32, (2 * SGU_CHUNK, SGU_CHUNK), 1)
        causal = (pos_s // CHUNK) <= (pos_t // CHUNK)
        vn = env["vn"]
        for p in range(PAIRS):
            wsp = jnp.where(causal, ws_ref[p], 0.0).astype(BF16)
            lanes = slice(p * LANES, (p + 1) * LANES)
            for c in range(0, ROWS, 2 * SGU_CHUNK):
                rows = [slice(c + j * SGU_CHUNK, c + (j + 1) * SGU_CHUNK) for j in range(2)]
                r = jnp.dot(wsp, jnp.concatenate([vn[rows[0], lanes], vn[rows[1], lanes]], axis=1),
                            preferred_element_type=F32)
                for j in range(2):
                    rj = r[:, j * LANES:(j + 1) * LANES]
                    mixed_ref[rows[j], lanes] = jnp.where(
                        first_half, rj[0:SGU_CHUNK], rj[SGU_CHUNK:2 * SGU_CHUNK])

    def step_sza():
        env["sz_a"] = jax.nn.silu(proj(c_za, c_ub))

    def step_ga():
        env["gate_a"] = jax.nn.sigmoid(proj(c_ga, c_gb))

    def step_pb():
        bias = jnp.concatenate([bs_ref[...]] * (ROWS // SGU_CHUNK), axis=0)
        y_b = env["ug"] * (mixed_ref[...] + bias) * env["sz_b"]
        env["p_b"] = jnp.dot(y_b.astype(BF16), wub_ref[...], preferred_element_type=F32)

    def step_gb():
        env["merged_b"] = jax.nn.sigmoid(proj(c_gb, D_REST)) * env["p_b"]

    dense = [step_vg, step_mu, step_ug, step_var, step_szb, step_mix, step_sza, step_ga, step_pb,
             step_gb]

    chains = [(tile, p) for tile in range(TILES) for p in range(PAIRS)]
    starts = [pl.multiple_of(window_start(tile), QT) for tile in range(TILES)]
    zs, csums, least = {}, {}, {}

    def stage_scores(c):
        tile, p = chains[c]
        visible = col - row < t0 + tile * QT - starts[tile]
        zs[c] = scores(p, starts[tile], q_chain(p, tile * QT), visible)

    def stage_stick(c):
        csums[c] = stick(zs[c])

    def stage_values(c):
        tile, p = chains[c]
        pv, tot = values(p, starts[tile], zs[c], csums[c], None)
        acc_ref[c] = pv
        carry_ref[c] = tot
        least[c] = jnp.min(tot, axis=0, keepdims=True)

    groups = [range(g, g + GROUP) for g in range(0, CHAINS, GROUP)]
    stages = (stage_scores, stage_stick, stage_values)
    dense_iter = iter(dense)
    for slot in range(len(groups) + len(stages) - 1):
        for s, stage in enumerate(stages):
            g = slot - s
            if 0 <= g < len(groups):
                for c in groups[g]:
                    stage(c)
        for _ in range(DENSE_PER_SLOT):
            step = next(dense_iter, None)
            if step is not None:
                step()
    for step in dense_iter:
        step()
    merged_b, gate_a, sz_a = env["merged_b"], env["gate_a"], env["sz_a"]

    for c, (tile, p) in enumerate(chains):
        more_ref[c] = jnp.logical_and(starts[tile] > 0,
                                      least[c][0, 0] < STICK_GONE).astype(jnp.int32)

    def older_keys(c, _):
        @pl.when(more_ref[c] != 0)
        def _():
            tile = c // PAIRS
            p = c % PAIRS
            q2 = q_chain(p, pl.multiple_of(tile * QT, QT))

            def body(state):
                end, _ = state
                start = pl.multiple_of(jnp.maximum(end - BLK, 0), QT)
                z = scores(p, start, q2, col < end - start)
                pv, tot = values(p, start, z, stick(z), carry_ref[c])
                acc_ref[c] += pv
                carry = carry_ref[c] + tot
                carry_ref[c] = carry
                return start, jnp.logical_and(start > 0, jnp.min(carry) < STICK_GONE)

            lax.while_loop(lambda state: state[1], body, (window_start(tile), True))
        return 0

    lax.fori_loop(0, CHAINS, older_keys, 0)

    o = jnp.concatenate(
        [jnp.concatenate(
            [jnp.where(first_half, acc_ref[tile * PAIRS + p, 0:QT], acc_ref[tile * PAIRS + p, QT:2 * QT])
             for p in range(PAIRS)], axis=1)
         for tile in range(TILES)], axis=0)
    p_a = jnp.dot((o * sz_a).astype(BF16), wua_ref[...], preferred_element_type=F32)
    merged = (gate_a * p_a + merged_b).astype(BF16)
    for r0 in range(0, ROWS, ROWS // 2):
        rows = slice(r0, r0 + ROWS // 2)
        y = x[rows] + jnp.dot(merged[rows], wo_ref[...], preferred_element_type=F32)
        ms = jnp.mean(y * y, axis=-1, keepdims=True)
        out_ref[0, rows, :] = y * lax.rsqrt(ms + EPS) * fg_ref[...]


def _block(x, q, k, v, tri2, norm_g, w_in16, ln_g, ln_b, ws, bs, gm2, w_up_a, w_up_b, w_out, final_g):
    b, s, _ = x.shape
    const = lambda shape: pl.BlockSpec(shape, lambda bi, i: (0,) * len(shape),
                                       pipeline_mode=pl.Buffered(1))
    seq_spec = pl.BlockSpec((PAIRS, 1, s, LANES), lambda bi, i: (0, bi, 0, 0))
    row_spec = pl.BlockSpec((1, ROWS, D_MODEL), lambda bi, i: (bi, i, 0))
    return pl.pallas_call(
        _block_kernel,
        grid=(b, s // ROWS),
        in_specs=[row_spec,
                  pl.BlockSpec((PAIRS, 1, ROWS, LANES), lambda bi, i: (0, bi, i, 0)),
                  seq_spec, seq_spec,
                  const((2 * BLK, BLK)),
                  const((1, D_MODEL)),
                  pl.BlockSpec((pl.Element(D_MODEL), pl.Element(D_REST)), lambda bi, i: (0, D_QKV),
                               pipeline_mode=pl.Buffered(1)),
                  const((1, D_SGU)), const((1, D_SGU)),
                  const((PAIRS, 2 * SGU_CHUNK, SGU_CHUNK)),
                  const((SGU_CHUNK, D_SGU)),
                  const((2 * D_SGU, LANES)),
                  const((D_SB, D_MODEL)), const((D_SGU, D_MODEL)), const((D_MODEL, D_MODEL)),
                  const((1, D_MODEL))],
        out_specs=row_spec,
        out_shape=jax.ShapeDtypeStruct((b, s, D_MODEL), F32),
        scratch_shapes=[pltpu.VMEM((CHAINS, 2 * QT, LANES), F32),
                        pltpu.VMEM((CHAINS, 2 * QT, LANES), F32),
                        pltpu.VMEM((ROWS, D_SGU), F32),
                        pltpu.SMEM((CHAINS,), jnp.int32)],
        compiler_params=pltpu.CompilerParams(
            dimension_semantics=("arbitrary", "arbitrary"), vmem_limit_bytes=VMEM_LIMIT),
        name="block",
    )(x, q, k, v, tri2, norm_g, w_in16, ln_g, ln_b, ws, bs, gm2, w_up_a, w_up_b, w_out, final_g)


def _layer(x, norm_g, w_in, ln_g, ln_b, w_s, b_s, w_up_a, w_up_b, w_out, final_g):
    b, s, _ = x.shape
    w_in16 = w_in.astype(BF16)
    norm_g = norm_g.reshape(1, D_MODEL)
    q, k, v = _qkv_proj(x.reshape(b * s, D_MODEL), norm_g, w_in16)

    j = jnp.arange(BLK)
    tri = (j[:, None] >= j[None, :]).astype(BF16)
    tri2 = jnp.concatenate([tri, tri], axis=0)
    pair_major = lambda a: a.reshape(PAIRS, b, s, LANES)

    gmean = (jnp.arange(D_SGU)[:, None] // GROUP_DIM == jnp.arange(LANES)[None, :] % N_GROUPS)
    gmean = (gmean.astype(F32) / GROUP_DIM).astype(BF16)
    gm2 = jnp.concatenate([gmean, gmean], axis=0)
    bias = jnp.repeat(b_s.T, GROUP_DIM, axis=1)
    return _block(x, pair_major(q), pair_major(k), pair_major(v), tri2, norm_g, w_in16,
                  ln_g.reshape(1, D_SGU), ln_b.reshape(1, D_SGU),
                  w_s.reshape(PAIRS, 2 * SGU_CHUNK, SGU_CHUNK), bias, gm2,
                  w_up_a.astype(BF16), w_up_b.astype(BF16), w_out.astype(BF16),
                  final_g.reshape(1, D_MODEL))


def kernel(x, norm_g, w_in, sgu_ln_g, sgu_ln_b, w_spatial, b_spatial, w_up_a, w_up_b, w_out,
           final_norm_g):
    assert norm_g.shape[0] == 1, "single-layer block"
    assert x.shape[1] % ROWS == 0 and (x.shape[0] * x.shape[1]) % ROWS_PROJ == 0
    return _layer(x, norm_g[0], w_in[0], sgu_ln_g[0], sgu_ln_b[0], w_spatial[0], b_spatial[0],
                  w_up_a[0], w_up_b[0], w_out[0], final_norm_g)
```

```python
import jax
import jax.numpy as jnp
from jax import lax
from jax.experimental import pallas as pl
from jax.experimental.pallas import tpu as pltpu

D_MODEL = 1024
N_HEADS = 8
HEAD_DIM = 64
D_SB = N_HEADS * HEAD_DIM
N_GROUPS = 8
GROUP_DIM = 64
D_SGU = N_GROUPS * GROUP_DIM
SGU_CHUNK = 128
CHUNK = 64
EPS = 1e-6
D_QKV = 3 * D_SB
D_REST = D_SB + 3 * D_SGU + 2 * D_MODEL

LANES = 128
PAIRS = D_SB // LANES

ROWS_PROJ = 1024
ROWS_TAIL = 1024
SUB_TAIL = 512
BLK = 256
QROWS = 512
QT = 128
TILES = QROWS // QT
CHAINS = TILES * PAIRS

VMEM_LIMIT = 56 * 1024 * 1024

F32 = jnp.float32
BF16 = jnp.bfloat16


def _rms_h(x, g):
    ms = jnp.mean(x * x, axis=-1, keepdims=True)
    return (x * lax.rsqrt(ms + EPS) * g).astype(BF16)


def _split_bf16(a):
    hi = a.astype(BF16)
    lo = (a - hi.astype(F32)).astype(BF16)
    return jnp.concatenate([hi, lo], axis=-1)


def _qkv_kernel(x_ref, g_ref, wqkv_ref, wrest_ref, wua_ref, wub_ref, wo_ref,
                q_ref, k_ref, v_ref, wrest16_ref, wua16_ref, wub16_ref, wo16_ref, wqkv16_ref):
    @pl.when(pl.program_id(0) == 0)
    def _():
        wqkv16_ref[...] = wqkv_ref[...].astype(BF16)

    for src, dst in ((wrest_ref, wrest16_ref), (wua_ref, wua16_ref), (wub_ref, wub16_ref),
                     (wo_ref, wo16_ref)):
        dst[...] = src[...].astype(BF16)

    h = _rms_h(x_ref[...], g_ref[...])
    scale = HEAD_DIM ** -0.5
    for j, (ref, mul) in enumerate(((q_ref, scale), (k_ref, None), (v_ref, None))):
        y = jnp.dot(h, wqkv16_ref[:, j * D_SB:(j + 1) * D_SB], preferred_element_type=F32)
        if mul is not None:
            y = y * mul
        y = y.astype(BF16)
        for p in range(PAIRS):
            ref[p] = y[:, p * LANES:(p + 1) * LANES]


def _qkv_proj(x2, norm_g, w_in, w_up_a, w_up_b, w_out):
    n = x2.shape[0]
    steps = n // ROWS_PROJ
    cols = D_REST // steps
    rows_up, rows_out = D_SB // steps, D_MODEL // steps
    assert cols % LANES == 0 and D_QKV % cols == 0 and rows_up % 16 == 0, "bf16 slab tiling"
    out = jax.ShapeDtypeStruct((PAIRS, n, LANES), BF16)
    row_spec = pl.BlockSpec((ROWS_PROJ, D_MODEL), lambda i: (i, 0))
    out_spec = pl.BlockSpec((PAIRS, ROWS_PROJ, LANES), lambda i: (0, i, 0))
    up_spec = pl.BlockSpec((rows_up, D_MODEL), lambda i: (i, 0))
    wout_spec = pl.BlockSpec((rows_out, D_MODEL), lambda i: (i, 0))
    return pl.pallas_call(
        _qkv_kernel,
        grid=(steps,),
        in_specs=[row_spec,
                  pl.BlockSpec((1, D_MODEL), lambda i: (0, 0)),
                  pl.BlockSpec((D_MODEL, D_QKV), lambda i: (0, 0), pipeline_mode=pl.Buffered(1)),
                  pl.BlockSpec((D_MODEL, cols), lambda i: (0, D_QKV // cols + i)),
                  up_spec, up_spec, wout_spec],
        out_specs=[out_spec, out_spec, out_spec,
                   pl.BlockSpec((D_MODEL, cols), lambda i: (0, i)),
                   up_spec, up_spec, wout_spec],
        out_shape=[out, out, out,
                   jax.ShapeDtypeStruct((D_MODEL, D_REST), BF16),
                   jax.ShapeDtypeStruct((D_SB, D_MODEL), BF16),
                   jax.ShapeDtypeStruct((D_SGU, D_MODEL), BF16),
                   jax.ShapeDtypeStruct((D_MODEL, D_MODEL), BF16)],
        scratch_shapes=[pltpu.VMEM((D_MODEL, D_QKV), BF16)],
        compiler_params=pltpu.CompilerParams(
            dimension_semantics=("arbitrary",), vmem_limit_bytes=VMEM_LIMIT),
        name="qkv_proj",
    )(x2, norm_g, w_in, w_in, w_up_a, w_up_b, w_out)


STICK_GONE = 88.0
MASKED = -1e30
LOG2E = 1.4426950408889634


def _softplus(z):
    return jnp.maximum(z, 0.0) + jnp.log(1.0 + jnp.exp2(jnp.abs(z) * -LOG2E))


def _attn_kernel(q_ref, k_ref, v_ref, tri_ref, o_ref, acc_ref, carry_ref, more_ref):
    t0 = pl.program_id(1) * QROWS
    lane = lax.broadcasted_iota(jnp.int32, (QT, LANES), 1)
    first_head = lane < HEAD_DIM
    row = lax.broadcasted_iota(jnp.int32, (2 * QT, BLK), 0) & (QT - 1)
    col = lax.broadcasted_iota(jnp.int32, (2 * QT, BLK), 1)

    def q_chain(p, r0):
        qt = q_ref[p, 0, pl.ds(r0, QT), :]
        zero = jnp.zeros_like(qt)
        return jnp.concatenate([jnp.where(first_head, qt, zero), jnp.where(first_head, zero, qt)],
                               axis=0)

    def scores(p, start, q2, visible, first_keys_visible=False):
        kb = k_ref[p, 0, pl.ds(start, BLK), :]
        z = lax.dot_general(q2, kb, (((1,), (1,)), ((), ())), preferred_element_type=F32)
        if first_keys_visible:
            return jnp.concatenate(
                [z[:, :QT], jnp.where(visible[:, QT:], z[:, QT:], MASKED)], axis=1)
        return jnp.where(visible, z, MASKED)

    def stick(z):
        csum = jnp.dot(_split_bf16(_softplus(z)), tri_ref[...], preferred_element_type=F32)
        return z - csum, csum[:, 0:1]

    def values(p, start, log_w, total, carry):
        if carry is not None:
            log_w = log_w - jnp.concatenate([carry] * (BLK // LANES), axis=1)
        vb = v_ref[p, 0, pl.ds(start, BLK), :]
        pv = jnp.dot(jnp.exp(log_w).astype(BF16), vb, preferred_element_type=F32)
        return pv, jnp.broadcast_to(total, (2 * QT, LANES))

    def window_start(tile):
        return jnp.maximum(t0 + (tile + 1) * QT - BLK, 0)

    chains = [(tile, p) for tile in range(TILES) for p in range(PAIRS)]
    starts = [pl.multiple_of(window_start(tile), QT) for tile in range(TILES)]
    zs = []
    for tile, p in chains:
        visible = col - row < t0 + tile * QT - starts[tile]
        zs.append(scores(p, starts[tile], q_chain(p, tile * QT), visible, tile >= 1))
    sticks = [stick(z) for z in zs]
    least = []
    for c, (tile, p) in enumerate(chains):
        pv, tot = values(p, starts[tile], *sticks[c], None)
        acc_ref[c] = pv
        carry_ref[c] = tot
        least.append(jnp.min(tot, axis=0, keepdims=True))
    for c, (tile, p) in enumerate(chains):
        more_ref[c] = jnp.logical_and(starts[tile] > 0,
                                      least[c][0, 0] < STICK_GONE).astype(jnp.int32)

    def older_keys(c, _):
        @pl.when(more_ref[c] != 0)
        def _():
            tile = c // PAIRS
            p = c % PAIRS
            q2 = q_chain(p, pl.multiple_of(tile * QT, QT))

            def body(state):
                end, _ = state
                start = pl.multiple_of(jnp.maximum(end - BLK, 0), QT)
                pv, tot = values(p, start, *stick(scores(p, start, q2, col < end - start)),
                                 carry_ref[c])
                acc_ref[c] += pv
                carry = carry_ref[c] + tot
                carry_ref[c] = carry
                return start, jnp.logical_and(start > 0, jnp.min(carry) < STICK_GONE)

            lax.while_loop(lambda state: state[1], body, (window_start(tile), True))
        return 0

    lax.fori_loop(0, CHAINS, older_keys, 0)

    for tile in range(TILES):
        for p in range(PAIRS):
            a = acc_ref[tile * PAIRS + p]
            o_ref[0, tile * QT:(tile + 1) * QT, p * LANES:(p + 1) * LANES] = jnp.where(
                first_head, a[0:QT], a[QT:2 * QT])


def _attention(q, k, v, tri):
    _, b, s, _ = q.shape
    seq_spec = pl.BlockSpec((PAIRS, 1, s, LANES), lambda bi, i: (0, bi, 0, 0))
    return pl.pallas_call(
        _attn_kernel,
        grid=(b, s // QROWS),
        in_specs=[pl.BlockSpec((PAIRS, 1, QROWS, LANES), lambda bi, i: (0, bi, i, 0)),
                  seq_spec, seq_spec,
                  pl.BlockSpec((2 * BLK, BLK), lambda bi, i: (0, 0))],
        out_specs=pl.BlockSpec((1, QROWS, D_SB), lambda bi, i: (bi, i, 0)),
        out_shape=jax.ShapeDtypeStruct((b, s, D_SB), F32),
        scratch_shapes=[pltpu.VMEM((CHAINS, 2 * QT, LANES), F32),
                        pltpu.VMEM((CHAINS, 2 * QT, LANES), F32),
                        pltpu.SMEM((CHAINS,), jnp.int32)],
        compiler_params=pltpu.CompilerParams(
            dimension_semantics=("arbitrary", "arbitrary"),
            vmem_limit_bytes=VMEM_LIMIT),
        name="stickbreak",
    )(q, k, v, tri)


def _tail_kernel(x_ref, o_ref, ng_ref, w_ref, lng_ref, lnb_ref, ws_ref, bs_ref, gm_ref,
                 wua_ref, wub_ref, wo_ref, fg_ref, out_ref, mixed_ref):
    tiles = [slice(r0, r0 + SUB_TAIL) for r0 in range(0, ROWS_TAIL, SUB_TAIL)]
    c_za, c_ub, c_vb, c_zb, c_ga, c_gb = 0, 512, 1024, 1536, 2048, 3072
    xs = [x_ref[t, :] for t in tiles]
    hs = [_rms_h(x, ng_ref[...]) for x in xs]

    def proj(h, lo, hi):
        return jnp.dot(h, w_ref[:, lo:hi], preferred_element_type=F32)

    half = lax.broadcasted_iota(jnp.int32, (SUB_TAIL, LANES), 1) // GROUP_DIM

    def spread(stats):
        return jnp.concatenate(
            [jnp.take_along_axis(stats, 2 * p + half, axis=1) for p in range(PAIRS)], axis=1)

    vgs = [jax.nn.gelu(proj(h, c_vb, c_zb)) for h in hs]
    mus = [spread(jnp.dot(_split_bf16(vg), gm_ref[...], preferred_element_type=F32)) for vg in vgs]
    ugs = [jax.nn.gelu(proj(h, c_ub, c_vb)) for h in hs]
    ds = [vg - mu for vg, mu in zip(vgs, mus)]
    vars_ = [spread(jnp.dot((d * d).astype(BF16), gm_ref[0:D_SGU, :], preferred_element_type=F32))
             for d in ds]
    sz_bs = [jax.nn.silu(proj(h, c_zb, c_ga)) for h in hs]
    vns = [(d * lax.rsqrt(var + EPS) * lng_ref[...] + lnb_ref[...]).astype(BF16)
           for d, var in zip(ds, vars_)]
    sz_as = [jax.nn.silu(proj(h, c_za, c_ub)) for h in hs]

    pos_t = lax.broadcasted_iota(jnp.int32, (2 * SGU_CHUNK, SGU_CHUNK), 0) % SGU_CHUNK
    pos_s = lax.broadcasted_iota(jnp.int32, (2 * SGU_CHUNK, SGU_CHUNK), 1)
    causal = (pos_s // CHUNK) <= (pos_t // CHUNK)
    lane = lax.broadcasted_iota(jnp.int32, (SGU_CHUNK, LANES), 1)
    first_group = lane < GROUP_DIM
    for p in range(PAIRS):
        wsp = jnp.where(causal, ws_ref[p], 0.0).astype(BF16)
        lanes = slice(p * LANES, (p + 1) * LANES)
        for t, vn in zip(tiles, vns):
            for c in range(0, SUB_TAIL, 2 * SGU_CHUNK):
                rows = [slice(c + j * SGU_CHUNK, c + (j + 1) * SGU_CHUNK) for j in range(2)]
                r = jnp.dot(wsp, jnp.concatenate([vn[rows[0], lanes], vn[rows[1], lanes]], axis=1),
                            preferred_element_type=F32)
                for j in range(2):
                    rj = r[:, j * LANES:(j + 1) * LANES]
                    mixed_ref[t.start + rows[j].start:t.start + rows[j].stop, lanes] = jnp.where(
                        first_group, rj[0:SGU_CHUNK], rj[SGU_CHUNK:2 * SGU_CHUNK])

    p_as = [jnp.dot((o_ref[t, :] * sz_a).astype(BF16), wua_ref[...], preferred_element_type=F32)
            for t, sz_a in zip(tiles, sz_as)]
    merged = [jax.nn.sigmoid(proj(h, c_ga, c_gb)) * p_a for h, p_a in zip(hs, p_as)]

    bias = jnp.concatenate([bs_ref[...]] * (SUB_TAIL // SGU_CHUNK), axis=0)
    p_bs = [jnp.dot((ug * (mixed_ref[t, :] + bias) * sz_b).astype(BF16), wub_ref[...],
                    preferred_element_type=F32)
            for t, ug, sz_b in zip(tiles, ugs, sz_bs)]
    merged = [m + jax.nn.sigmoid(proj(h, c_gb, D_REST)) * p_b
              for m, h, p_b in zip(merged, hs, p_bs)]

    ys = [x + jnp.dot(m.astype(BF16), wo_ref[...], preferred_element_type=F32)
          for x, m in zip(xs, merged)]
    for t, y in zip(tiles, ys):
        ms = jnp.mean(y * y, axis=-1, keepdims=True)
        out_ref[t, :] = y * lax.rsqrt(ms + EPS) * fg_ref[...]


def _tail(x2, o2, norm_g, w_rest, ln_g, ln_b, ws, bs, gm2, w_up_a, w_up_b, w_out, final_g):
    n = x2.shape[0]
    const = lambda shape: pl.BlockSpec(shape, lambda i: (0,) * len(shape),
                                       pipeline_mode=pl.Buffered(1))
    return pl.pallas_call(
        _tail_kernel,
        grid=(n // ROWS_TAIL,),
        in_specs=[pl.BlockSpec((ROWS_TAIL, D_MODEL), lambda i: (i, 0)),
                  pl.BlockSpec((ROWS_TAIL, D_SB), lambda i: (i, 0)),
                  const((1, D_MODEL)),
                  const((D_MODEL, D_REST)),
                  const((1, D_SGU)), const((1, D_SGU)),
                  const((PAIRS, 2 * SGU_CHUNK, SGU_CHUNK)),
                  const((SGU_CHUNK, D_SGU)),
                  const((2 * D_SGU, LANES)),
                  const((D_SB, D_MODEL)), const((D_SGU, D_MODEL)), const((D_MODEL, D_MODEL)),
                  const((1, D_MODEL))],
        out_specs=pl.BlockSpec((ROWS_TAIL, D_MODEL), lambda i: (i, 0)),
        out_shape=jax.ShapeDtypeStruct((n, D_MODEL), F32),
        scratch_shapes=[pltpu.VMEM((ROWS_TAIL, D_SGU), F32)],
        compiler_params=pltpu.CompilerParams(
            dimension_semantics=("arbitrary",), vmem_limit_bytes=VMEM_LIMIT),
        name="tail",
    )(x2, o2, norm_g, w_rest, ln_g, ln_b, ws, bs, gm2, w_up_a, w_up_b, w_out, final_g)


def _layer(x, norm_g, w_in, ln_g, ln_b, w_s, b_s, w_up_a, w_up_b, w_out, final_g):
    b, s, _ = x.shape
    x2 = x.reshape(b * s, D_MODEL)
    q, k, v, w_rest16, w_up_a16, w_up_b16, w_out16 = _qkv_proj(
        x2, norm_g.reshape(1, D_MODEL), w_in, w_up_a, w_up_b, w_out)

    j = jnp.arange(BLK)
    tri = (j[:, None] >= j[None, :]).astype(BF16)
    tri2 = jnp.concatenate([tri, tri], axis=0)
    pair_major = lambda a: a.reshape(PAIRS, b, s, LANES)
    o = _attention(pair_major(q), pair_major(k), pair_major(v), tri2)

    gmean = (jnp.arange(D_SGU)[:, None] // GROUP_DIM == jnp.arange(LANES)[None, :] % N_GROUPS)
    gmean = (gmean.astype(F32) / GROUP_DIM).astype(BF16)
    gm2 = jnp.concatenate([gmean, gmean], axis=0)
    bias = jnp.repeat(b_s.T, GROUP_DIM, axis=1)
    return _tail(x2, o.reshape(b * s, D_SB), norm_g.reshape(1, D_MODEL), w_rest16,
                 ln_g.reshape(1, D_SGU), ln_b.reshape(1, D_SGU),
                 w_s.reshape(PAIRS, 2 * SGU_CHUNK, SGU_CHUNK), bias, gm2,
                 w_up_a16, w_up_b16, w_out16,
                 final_g.reshape(1, D_MODEL)).reshape(b, s, D_MODEL)


def kernel(x, norm_g, w_in, sgu_ln_g, sgu_ln_b, w_spatial, b_spatial, w_up_a, w_up_b, w_out,
           final_norm_g):
    assert norm_g.shape[0] == 1, "single-layer block"
    assert x.shape[1] % QROWS == 0 and (x.shape[0] * x.shape[1]) % ROWS_PROJ == 0
    return _layer(x, norm_g[0], w_in[0], sgu_ln_g[0], sgu_ln_b[0], w_spatial[0], b_spatial[0],
                  w_up_a[0], w_up_b[0], w_out[0], final_norm_g)
```

```python
import jax
import jax.numpy as jnp
from jax import lax
from jax.experimental import pallas as pl
from jax.experimental.pallas import tpu as pltpu

D_MODEL = 1024
N_HEADS = 8
HEAD_DIM = 64
D_SB = N_HEADS * HEAD_DIM
N_GROUPS = 8
GROUP_DIM = 64
D_SGU = N_GROUPS * GROUP_DIM
SGU_CHUNK = 128
CHUNK = 64
EPS = 1e-6
D_QKV = 3 * D_SB
D_REST = D_SB + 3 * D_SGU + 2 * D_MODEL

LANES = 128
PAIRS = D_SB // LANES

ROWS_PROJ = 1024
ROWS_TAIL = 1024
SUB_TAIL = 512
BLK = 256
QROWS = 1024
QT = 128
TILES = QROWS // QT
CHAINS = TILES * PAIRS

VMEM_LIMIT = 56 * 1024 * 1024

F32 = jnp.float32
BF16 = jnp.bfloat16


def _rms_h(x, g):
    ms = jnp.mean(x * x, axis=-1, keepdims=True)
    return (x * lax.rsqrt(ms + EPS) * g).astype(BF16)


def _split_bf16(a):
    hi = a.astype(BF16)
    lo = (a - hi.astype(F32)).astype(BF16)
    return jnp.concatenate([hi, lo], axis=-1)


def _qkv_kernel(x_ref, g_ref, wqkv_ref, wrest_ref, wua_ref, wub_ref, wo_ref,
                q_ref, k_ref, v_ref, wrest16_ref, wua16_ref, wub16_ref, wo16_ref, wqkv16_ref):
    @pl.when(pl.program_id(0) == 0)
    def _():
        wqkv16_ref[...] = wqkv_ref[...].astype(BF16)

    for src, dst in ((wrest_ref, wrest16_ref), (wua_ref, wua16_ref), (wub_ref, wub16_ref),
                     (wo_ref, wo16_ref)):
        dst[...] = src[...].astype(BF16)

    h = _rms_h(x_ref[...], g_ref[...])
    scale = HEAD_DIM ** -0.5
    for j, (ref, mul) in enumerate(((q_ref, scale), (k_ref, None), (v_ref, None))):
        y = jnp.dot(h, wqkv16_ref[:, j * D_SB:(j + 1) * D_SB], preferred_element_type=F32)
        if mul is not None:
            y = y * mul
        y = y.astype(BF16)
        for p in range(PAIRS):
            ref[p] = y[:, p * LANES:(p + 1) * LANES]


def _qkv_proj(x2, norm_g, w_in, w_up_a, w_up_b, w_out):
    n = x2.shape[0]
    steps = n // ROWS_PROJ
    cols = D_REST // steps
    rows_up, rows_out = D_SB // steps, D_MODEL // steps
    assert cols % LANES == 0 and D_QKV % cols == 0 and rows_up % 16 == 0, "bf16 slab tiling"
    out = jax.ShapeDtypeStruct((PAIRS, n, LANES), BF16)
    row_spec = pl.BlockSpec((ROWS_PROJ, D_MODEL), lambda i: (i, 0))
    out_spec = pl.BlockSpec((PAIRS, ROWS_PROJ, LANES), lambda i: (0, i, 0))
    up_spec = pl.BlockSpec((rows_up, D_MODEL), lambda i: (i, 0))
    wout_spec = pl.BlockSpec((rows_out, D_MODEL), lambda i: (i, 0))
    return pl.pallas_call(
        _qkv_kernel,
        grid=(steps,),
        in_specs=[row_spec,
                  pl.BlockSpec((1, D_MODEL), lambda i: (0, 0)),
                  pl.BlockSpec((D_MODEL, D_QKV), lambda i: (0, 0), pipeline_mode=pl.Buffered(1)),
                  pl.BlockSpec((D_MODEL, cols), lambda i: (0, D_QKV // cols + i)),
                  up_spec, up_spec, wout_spec],
        out_specs=[out_spec, out_spec, out_spec,
                   pl.BlockSpec((D_MODEL, cols), lambda i: (0, i)),
                   up_spec, up_spec, wout_spec],
        out_shape=[out, out, out,
                   jax.ShapeDtypeStruct((D_MODEL, D_REST), BF16),
                   jax.ShapeDtypeStruct((D_SB, D_MODEL), BF16),
                   jax.ShapeDtypeStruct((D_SGU, D_MODEL), BF16),
                   jax.ShapeDtypeStruct((D_MODEL, D_MODEL), BF16)],
        scratch_shapes=[pltpu.VMEM((D_MODEL, D_QKV), BF16)],
        compiler_params=pltpu.CompilerParams(
            dimension_semantics=("arbitrary",), vmem_limit_bytes=VMEM_LIMIT),
        name="qkv_proj",
    )(x2, norm_g, w_in, w_in, w_up_a, w_up_b, w_out)


STICK_GONE = 88.0
MASKED = -1e30
LOG2E = 1.4426950408889634


def _softplus(z):
    return jnp.maximum(z, 0.0) + jnp.log(1.0 + jnp.exp2(jnp.abs(z) * -LOG2E))


def _attn_kernel(q_ref, k_ref, v_ref, tri_ref, o_ref, acc_ref, carry_ref, more_ref):
    t0 = pl.program_id(1) * QROWS
    lane = lax.broadcasted_iota(jnp.int32, (QT, LANES), 1)
    first_head = lane < HEAD_DIM
    row = lax.broadcasted_iota(jnp.int32, (2 * QT, BLK), 0) & (QT - 1)
    col = lax.broadcasted_iota(jnp.int32, (2 * QT, BLK), 1)

    def q_chain(p, r0):
        qt = q_ref[p, 0, pl.ds(r0, QT), :]
        zero = jnp.zeros_like(qt)
        return jnp.concatenate([jnp.where(first_head, qt, zero), jnp.where(first_head, zero, qt)],
                               axis=0)

    def scores(p, start, q2, visible, first_keys_visible=False):
        kb = k_ref[p, 0, pl.ds(start, BLK), :]
        z = lax.dot_general(q2, kb, (((1,), (1,)), ((), ())), preferred_element_type=F32)
        if first_keys_visible:
            return jnp.concatenate(
                [z[:, :QT], jnp.where(visible[:, QT:], z[:, QT:], MASKED)], axis=1)
        return jnp.where(visible, z, MASKED)

    def stick(z):
        csum = jnp.dot(_split_bf16(_softplus(z)), tri_ref[...], preferred_element_type=F32)
        return z - csum, csum[:, 0:1]

    def values(p, start, log_w, total, carry):
        if carry is not None:
            log_w = log_w - jnp.concatenate([carry] * (BLK // LANES), axis=1)
        vb = v_ref[p, 0, pl.ds(start, BLK), :]
        pv = jnp.dot(jnp.exp(log_w).astype(BF16), vb, preferred_element_type=F32)
        return pv, jnp.broadcast_to(total, (2 * QT, LANES))

    def window_start(tile):
        return jnp.maximum(t0 + (tile + 1) * QT - BLK, 0)

    chains = [(tile, p) for tile in range(TILES) for p in range(PAIRS)]
    starts = [pl.multiple_of(window_start(tile), QT) for tile in range(TILES)]
    zs = []
    for tile, p in chains:
        visible = col - row < t0 + tile * QT - starts[tile]
        zs.append(scores(p, starts[tile], q_chain(p, tile * QT), visible, tile >= 1))
    sticks = [stick(z) for z in zs]
    least = []
    for c, (tile, p) in enumerate(chains):
        pv, tot = values(p, starts[tile], *sticks[c], None)
        acc_ref[c] = pv
        carry_ref[c] = tot
        least.append(jnp.min(tot, axis=0, keepdims=True))
    n_more = jnp.int32(0)
    for c, (tile, p) in enumerate(chains):
        more_ref[n_more] = jnp.int32(c)
        listed = jnp.logical_and(starts[tile] > 0, least[c][0, 0] < STICK_GONE)
        n_more = n_more + listed.astype(jnp.int32)

    def older_keys(j, _):
        c = more_ref[j]
        tile = c // PAIRS
        p = c % PAIRS
        q2 = q_chain(p, pl.multiple_of(tile * QT, QT))

        def body(state):
            end, _ = state
            start = pl.multiple_of(jnp.maximum(end - BLK, 0), QT)
            pv, tot = values(p, start, *stick(scores(p, start, q2, col < end - start)),
                             carry_ref[c])
            acc_ref[c] += pv
            carry = carry_ref[c] + tot
            carry_ref[c] = carry
            return start, jnp.logical_and(start > 0, jnp.min(carry) < STICK_GONE)

        lax.while_loop(lambda state: state[1], body, (window_start(tile), True))
        return 0

    lax.fori_loop(0, n_more, older_keys, 0)

    for tile in range(TILES):
        for p in range(PAIRS):
            a = acc_ref[tile * PAIRS + p]
            o_ref[0, tile * QT:(tile + 1) * QT, p * LANES:(p + 1) * LANES] = jnp.where(
                first_head, a[0:QT], a[QT:2 * QT])


def _attention(q, k, v, tri):
    _, b, s, _ = q.shape
    seq_spec = pl.BlockSpec((PAIRS, 1, s, LANES), lambda bi, i: (0, bi, 0, 0))
    return pl.pallas_call(
        _attn_kernel,
        grid=(b, s // QROWS),
        in_specs=[pl.BlockSpec((PAIRS, 1, QROWS, LANES), lambda bi, i: (0, bi, i, 0)),
                  seq_spec, seq_spec,
                  pl.BlockSpec((2 * BLK, BLK), lambda bi, i: (0, 0))],
        out_specs=pl.BlockSpec((1, QROWS, D_SB), lambda bi, i: (bi, i, 0)),
        out_shape=jax.ShapeDtypeStruct((b, s, D_SB), F32),
        scratch_shapes=[pltpu.VMEM((CHAINS, 2 * QT, LANES), F32),
                        pltpu.VMEM((CHAINS, 2 * QT, LANES), F32),
                        pltpu.SMEM((CHAINS,), jnp.int32)],
        compiler_params=pltpu.CompilerParams(
            dimension_semantics=("arbitrary", "arbitrary"),
            vmem_limit_bytes=VMEM_LIMIT),
        name="stickbreak",
    )(q, k, v, tri)


def _tail_kernel(x_ref, o_ref, ng_ref, w_ref, lng_ref, lnb_ref, ws_ref, bs_ref, gm_ref,
                 wua_ref, wub_ref, wo_ref, fg_ref, out_ref, mixed_ref):
    tiles = [slice(r0, r0 + SUB_TAIL) for r0 in range(0, ROWS_TAIL, SUB_TAIL)]
    c_za, c_ub, c_vb, c_zb, c_ga, c_gb = 0, 512, 1024, 1536, 2048, 3072
    xs = [x_ref[t, :] for t in tiles]
    hs = [_rms_h(x, ng_ref[...]) for x in xs]

    def proj(h, lo, hi):
        return jnp.dot(h, w_ref[:, lo:hi], preferred_element_type=F32)

    half = lax.broadcasted_iota(jnp.int32, (SUB_TAIL, LANES), 1) // GROUP_DIM

    def spread(stats):
        return jnp.concatenate(
            [jnp.take_along_axis(stats, 2 * p + half, axis=1) for p in range(PAIRS)], axis=1)

    vgs = [jax.nn.gelu(proj(h, c_vb, c_zb)) for h in hs]
    mus = [spread(jnp.dot(_split_bf16(vg), gm_ref[...], preferred_element_type=F32)) for vg in vgs]
    ugs = [jax.nn.gelu(proj(h, c_ub, c_vb)) for h in hs]
    ds = [vg - mu for vg, mu in zip(vgs, mus)]
    vars_ = [spread(jnp.dot((d * d).astype(BF16), gm_ref[0:D_SGU, :], preferred_element_type=F32))
             for d in ds]
    sz_bs = [jax.nn.silu(proj(h, c_zb, c_ga)) for h in hs]
    vns = [(d * lax.rsqrt(var + EPS) * lng_ref[...] + lnb_ref[...]).astype(BF16)
           for d, var in zip(ds, vars_)]
    sz_as = [jax.nn.silu(proj(h, c_za, c_ub)) for h in hs]

    pos_t = lax.broadcasted_iota(jnp.int32, (2 * SGU_CHUNK, SGU_CHUNK), 0) % SGU_CHUNK
    pos_s = lax.broadcasted_iota(jnp.int32, (2 * SGU_CHUNK, SGU_CHUNK), 1)
    causal = (pos_s // CHUNK) <= (pos_t // CHUNK)
    lane = lax.broadcasted_iota(jnp.int32, (SGU_CHUNK, LANES), 1)
    first_group = lane < GROUP_DIM
    for p in range(PAIRS):
        wsp = jnp.where(causal, ws_ref[p], 0.0).astype(BF16)
        lanes = slice(p * LANES, (p + 1) * LANES)
        for t, vn in zip(tiles, vns):
            for c in range(0, SUB_TAIL, 2 * SGU_CHUNK):
                rows = [slice(c + j * SGU_CHUNK, c + (j + 1) * SGU_CHUNK) for j in range(2)]
                r = jnp.dot(wsp, jnp.concatenate([vn[rows[0], lanes], vn[rows[1], lanes]], axis=1),
                            preferred_element_type=F32)
                for j in range(2):
                    rj = r[:, j * LANES:(j + 1) * LANES]
                    mixed_ref[t.start + rows[j].start:t.start + rows[j].stop, lanes] = jnp.where(
                        first_group, rj[0:SGU_CHUNK], rj[SGU_CHUNK:2 * SGU_CHUNK])

    p_as = [jnp.dot((o_ref[t, :] * sz_a).astype(BF16), wua_ref[...], preferred_element_type=F32)
            for t, sz_a in zip(tiles, sz_as)]
    merged = [jax.nn.sigmoid(proj(h, c_ga, c_gb)) * p_a for h, p_a in zip(hs, p_as)]

    bias = jnp.concatenate([bs_ref[...]] * (SUB_TAIL // SGU_CHUNK), axis=0)
    p_bs = [jnp.dot((ug * (mixed_ref[t, :] + bias) * sz_b).astype(BF16), wub_ref[...],
                    preferred_element_type=F32)
            for t, ug, sz_b in zip(tiles, ugs, sz_bs)]
    merged = [m + jax.nn.sigmoid(proj(h, c_gb, D_REST)) * p_b
              for m, h, p_b in zip(merged, hs, p_bs)]

    ys = [x + jnp.dot(m.astype(BF16), wo_ref[...], preferred_element_type=F32)
          for x, m in zip(xs, merged)]
    for t, y in zip(tiles, ys):
        ms = jnp.mean(y * y, axis=-1, keepdims=True)
        out_ref[t, :] = y * lax.rsqrt(ms + EPS) * fg_ref[...]


def _tail(x2, o2, norm_g, w_rest, ln_g, ln_b, ws, bs, gm2, w_up_a, w_up_b, w_out, final_g):
    n = x2.shape[0]
    const = lambda shape: pl.BlockSpec(shape, lambda i: (0,) * len(shape),
                                       pipeline_mode=pl.Buffered(1))
    return pl.pallas_call(
        _tail_kernel,
        grid=(n // ROWS_TAIL,),
        in_specs=[pl.BlockSpec((ROWS_TAIL, D_MODEL), lambda i: (i, 0)),
                  pl.BlockSpec((ROWS_TAIL, D_SB), lambda i: (i, 0)),
                  const((1, D_MODEL)),
                  const((D_MODEL, D_REST)),
                  const((1, D_SGU)), const((1, D_SGU)),
                  const((PAIRS, 2 * SGU_CHUNK, SGU_CHUNK)),
                  const((SGU_CHUNK, D_SGU)),
                  const((2 * D_SGU, LANES)),
                  const((D_SB, D_MODEL)), const((D_SGU, D_MODEL)), const((D_MODEL, D_MODEL)),
                  const((1, D_MODEL))],
        out_specs=pl.BlockSpec((ROWS_TAIL, D_MODEL), lambda i: (i, 0)),
        out_shape=jax.ShapeDtypeStruct((n, D_MODEL), F32),
        scratch_shapes=[pltpu.VMEM((ROWS_TAIL, D_SGU), F32)],
        compiler_params=pltpu.CompilerParams(
            dimension_semantics=("arbitrary",), vmem_limit_bytes=VMEM_LIMIT),
        name="tail",
    )(x2, o2, norm_g, w_rest, ln_g, ln_b, ws, bs, gm2, w_up_a, w_up_b, w_out, final_g)


def _layer(x, norm_g, w_in, ln_g, ln_b, w_s, b_s, w_up_a, w_up_b, w_out, final_g):
    b, s, _ = x.shape
    x2 = x.reshape(b * s, D_MODEL)
    q, k, v, w_rest16, w_up_a16, w_up_b16, w_out16 = _qkv_proj(
        x2, norm_g.reshape(1, D_MODEL), w_in, w_up_a, w_up_b, w_out)

    j = jnp.arange(BLK)
    tri = (j[:, None] >= j[None, :]).astype(BF16)
    tri2 = jnp.concatenate([tri, tri], axis=0)
    pair_major = lambda a: a.reshape(PAIRS, b, s, LANES)
    o = _attention(pair_major(q), pair_major(k), pair_major(v), tri2)

    gmean = (jnp.arange(D_SGU)[:, None] // GROUP_DIM == jnp.arange(LANES)[None, :] % N_GROUPS)
    gmean = (gmean.astype(F32) / GROUP_DIM).astype(BF16)
    gm2 = jnp.concatenate([gmean, gmean], axis=0)
    bias = jnp.repeat(b_s.T, GROUP_DIM, axis=1)
    return _tail(x2, o.reshape(b * s, D_SB), norm_g.reshape(1, D_MODEL), w_rest16,
                 ln_g.reshape(1, D_SGU), ln_b.reshape(1, D_SGU),
                 w_s.reshape(PAIRS, 2 * SGU_CHUNK, SGU_CHUNK), bias, gm2,
                 w_up_a16, w_up_b16, w_out16,
                 final_g.reshape(1, D_MODEL)).reshape(b, s, D_MODEL)


def kernel(x, norm_g, w_in, sgu_ln_g, sgu_ln_b, w_spatial, b_spatial, w_up_a, w_up_b, w_out,
           final_norm_g):
    assert norm_g.shape[0] == 1, "single-layer block"
    assert x.shape[1] % QROWS == 0 and (x.shape[0] * x.shape[1]) % ROWS_PROJ == 0
    return _layer(x, norm_g[0], w_in[0], sgu_ln_g[0], sgu_ln_b[0], w_spatial[0], b_spatial[0],
                  w_up_a[0], w_up_b[0], w_out[0], final_norm_g)
```

```python
import jax
import jax.numpy as jnp
from jax import lax
from jax.experimental import pallas as pl
from jax.experimental.pallas import tpu as pltpu

D_MODEL = 1024
N_HEADS = 8
HEAD_DIM = 64
D_SB = N_HEADS * HEAD_DIM
N_GROUPS = 8
GROUP_DIM = 64
D_SGU = N_GROUPS * GROUP_DIM
SGU_CHUNK = 128
CHUNK = 64
EPS = 1e-6
D_QKV = 3 * D_SB
D_REST = D_SB + 3 * D_SGU + 2 * D_MODEL

LANES = 128
PAIRS = D_SB // LANES

ROWS_PROJ = 1024
ROWS_TAIL = 1024
SUB_TAIL = 512
BLK = 256
QROWS = 1024
QT = 128
TILES = QROWS // QT
CHAINS = TILES * PAIRS
GROUP = 8

VMEM_LIMIT = 56 * 1024 * 1024

F32 = jnp.float32
BF16 = jnp.bfloat16


def _rms_h(x, g):
    ms = jnp.mean(x * x, axis=-1, keepdims=True)
    return (x * lax.rsqrt(ms + EPS) * g).astype(BF16)


def _split_bf16(a):
    hi = a.astype(BF16)
    lo = (a - hi.astype(F32)).astype(BF16)
    return jnp.concatenate([hi, lo], axis=-1)


def _qkv_kernel(x_ref, g_ref, wqkv_ref, wrest_ref, wua_ref, wub_ref, wo_ref,
                q_ref, k_ref, v_ref, wrest16_ref, wua16_ref, wub16_ref, wo16_ref, wqkv16_ref):
    @pl.when(pl.program_id(0) == 0)
    def _():
        wqkv16_ref[...] = wqkv_ref[...].astype(BF16)

    for src, dst in ((wrest_ref, wrest16_ref), (wua_ref, wua16_ref), (wub_ref, wub16_ref),
                     (wo_ref, wo16_ref)):
        dst[...] = src[...].astype(BF16)

    h = _rms_h(x_ref[...], g_ref[...])
    scale = HEAD_DIM ** -0.5
    for j, (ref, mul) in enumerate(((q_ref, scale), (k_ref, None), (v_ref, None))):
        y = jnp.dot(h, wqkv16_ref[:, j * D_SB:(j + 1) * D_SB], preferred_element_type=F32)
        if mul is not None:
            y = y * mul
        y = y.astype(BF16)
        for p in range(PAIRS):
            ref[p] = y[:, p * LANES:(p + 1) * LANES]


def _qkv_proj(x2, norm_g, w_in, w_up_a, w_up_b, w_out):
    n = x2.shape[0]
    steps = n // ROWS_PROJ
    cols = D_REST // steps
    rows_up, rows_out = D_SB // steps, D_MODEL // steps
    assert cols % LANES == 0 and D_QKV % cols == 0 and rows_up % 16 == 0, "bf16 slab tiling"
    out = jax.ShapeDtypeStruct((PAIRS, n, LANES), BF16)
    row_spec = pl.BlockSpec((ROWS_PROJ, D_MODEL), lambda i: (i, 0))
    out_spec = pl.BlockSpec((PAIRS, ROWS_PROJ, LANES), lambda i: (0, i, 0))
    up_spec = pl.BlockSpec((rows_up, D_MODEL), lambda i: (i, 0))
    wout_spec = pl.BlockSpec((rows_out, D_MODEL), lambda i: (i, 0))
    return pl.pallas_call(
        _qkv_kernel,
        grid=(steps,),
        in_specs=[row_spec,
                  pl.BlockSpec((1, D_MODEL), lambda i: (0, 0)),
                  pl.BlockSpec((D_MODEL, D_QKV), lambda i: (0, 0), pipeline_mode=pl.Buffered(1)),
                  pl.BlockSpec((D_MODEL, cols), lambda i: (0, D_QKV // cols + i)),
                  up_spec, up_spec, wout_spec],
        out_specs=[out_spec, out_spec, out_spec,
                   pl.BlockSpec((D_MODEL, cols), lambda i: (0, i)),
                   up_spec, up_spec, wout_spec],
        out_shape=[out, out, out,
                   jax.ShapeDtypeStruct((D_MODEL, D_REST), BF16),
                   jax.ShapeDtypeStruct((D_SB, D_MODEL), BF16),
                   jax.ShapeDtypeStruct((D_SGU, D_MODEL), BF16),
                   jax.ShapeDtypeStruct((D_MODEL, D_MODEL), BF16)],
        scratch_shapes=[pltpu.VMEM((D_MODEL, D_QKV), BF16)],
        compiler_params=pltpu.CompilerParams(
            dimension_semantics=("arbitrary",), vmem_limit_bytes=VMEM_LIMIT),
        name="qkv_proj",
    )(x2, norm_g, w_in, w_in, w_up_a, w_up_b, w_out)


STICK_GONE = 88.0
MASKED = -1e30
SOFTPLUS_CLAMP = 40.0


def _softplus(z):
    return jnp.maximum(z, jnp.log(1.0 + jnp.exp(jnp.minimum(z, SOFTPLUS_CLAMP))))


def _attn_kernel(q_ref, k_ref, v_ref, tri_ref, o_ref, acc_ref, carry_ref, more_ref):
    t0 = pl.program_id(1) * QROWS
    lane = lax.broadcasted_iota(jnp.int32, (QT, LANES), 1)
    first_head = lane < HEAD_DIM
    row = lax.broadcasted_iota(jnp.int32, (2 * QT, BLK), 0) & (QT - 1)
    col = lax.broadcasted_iota(jnp.int32, (2 * QT, BLK), 1)

    def q_chain(p, r0):
        qt = q_ref[p, 0, pl.ds(r0, QT), :]
        zero = jnp.zeros_like(qt)
        return jnp.concatenate([jnp.where(first_head, qt, zero), jnp.where(first_head, zero, qt)],
                               axis=0)

    def scores(p, start, q2, visible, first_keys_visible=False):
        kb = k_ref[p, 0, pl.ds(start, BLK), :]
        z = lax.dot_general(q2, kb, (((1,), (1,)), ((), ())), preferred_element_type=F32)
        if first_keys_visible:
            return jnp.concatenate(
                [z[:, :QT], jnp.where(visible[:, QT:], z[:, QT:], MASKED)], axis=1)
        return jnp.where(visible, z, MASKED)

    def stick(z):
        csum = jnp.dot(_split_bf16(_softplus(z)), tri_ref[...], preferred_element_type=F32)
        return z - csum, csum[:, 0:1]

    def values(p, start, log_w, total, carry):
        if carry is not None:
            log_w = log_w - jnp.concatenate([carry] * (BLK // LANES), axis=1)
        vb = v_ref[p, 0, pl.ds(start, BLK), :]
        pv = jnp.dot(jnp.exp(log_w).astype(BF16), vb, preferred_element_type=F32)
        return pv, jnp.broadcast_to(total, (2 * QT, LANES))

    def window_start(tile):
        return jnp.maximum(t0 + (tile + 1) * QT - BLK, 0)

    chains = [(tile, p) for tile in range(TILES) for p in range(PAIRS)]
    starts = [pl.multiple_of(window_start(tile), QT) for tile in range(TILES)]
    zs, sticks, least = {}, {}, {}

    def stage_scores(c):
        tile, p = chains[c]
        visible = col - row < t0 + tile * QT - starts[tile]
        zs[c] = scores(p, starts[tile], q_chain(p, tile * QT), visible, tile >= 1)

    def stage_stick(c):
        sticks[c] = stick(zs[c])

    def stage_values(c):
        tile, p = chains[c]
        pv, tot = values(p, starts[tile], *sticks[c], None)
        acc_ref[c] = pv
        carry_ref[c] = tot
        least[c] = jnp.min(tot, axis=0, keepdims=True)

    stages = (stage_scores, stage_stick, stage_values)
    n_groups = CHAINS // GROUP
    for slot in range(n_groups + len(stages) - 1):
        for s, stage in enumerate(stages):
            g = slot - s
            if 0 <= g < n_groups:
                for c in range(g * GROUP, (g + 1) * GROUP):
                    stage(c)
    n_more = jnp.int32(0)
    for c, (tile, p) in enumerate(chains):
        more_ref[n_more] = jnp.int32(c)
        listed = jnp.logical_and(starts[tile] > 0, least[c][0, 0] < STICK_GONE)
        n_more = n_more + listed.astype(jnp.int32)

    def older_keys(j, _):
        c = more_ref[j]
        tile = c // PAIRS
        p = c % PAIRS
        q2 = q_chain(p, pl.multiple_of(tile * QT, QT))

        def body(state):
            end, _ = state
            start = pl.multiple_of(jnp.maximum(end - BLK, 0), QT)
            pv, tot = values(p, start, *stick(scores(p, start, q2, col < end - start)),
                             carry_ref[c])
            acc_ref[c] += pv
            carry = carry_ref[c] + tot
            carry_ref[c] = carry
            return start, jnp.logical_and(start > 0, jnp.min(carry) < STICK_GONE)

        lax.while_loop(lambda state: state[1], body, (window_start(tile), True))
        return 0

    lax.fori_loop(0, n_more, older_keys, 0)

    for tile in range(TILES):
        for p in range(PAIRS):
            a = acc_ref[tile * PAIRS + p]
            o_ref[0, tile * QT:(tile + 1) * QT, p * LANES:(p + 1) * LANES] = jnp.where(
                first_head, a[0:QT], a[QT:2 * QT])


def _attention(q, k, v, tri):
    _, b, s, _ = q.shape
    seq_spec = pl.BlockSpec((PAIRS, 1, s, LANES), lambda bi, i: (0, bi, 0, 0))
    return pl.pallas_call(
        _attn_kernel,
        grid=(b, s // QROWS),
        in_specs=[pl.BlockSpec((PAIRS, 1, QROWS, LANES), lambda bi, i: (0, bi, i, 0)),
                  seq_spec, seq_spec,
                  pl.BlockSpec((2 * BLK, BLK), lambda bi, i: (0, 0))],
        out_specs=pl.BlockSpec((1, QROWS, D_SB), lambda bi, i: (bi, i, 0)),
        out_shape=jax.ShapeDtypeStruct((b, s, D_SB), F32),
        scratch_shapes=[pltpu.VMEM((CHAINS, 2 * QT, LANES), F32),
                        pltpu.VMEM((CHAINS, 2 * QT, LANES), F32),
                        pltpu.SMEM((CHAINS,), jnp.int32)],
        compiler_params=pltpu.CompilerParams(
            dimension_semantics=("arbitrary", "arbitrary"),
            vmem_limit_bytes=VMEM_LIMIT),
        name="stickbreak",
    )(q, k, v, tri)


def _tail_kernel(x_ref, o_ref, ng_ref, w_ref, lng_ref, lnb_ref, ws_ref, bs_ref, gm_ref,
                 wua_ref, wub_ref, wo_ref, fg_ref, out_ref, mixed_ref):
    tiles = [slice(r0, r0 + SUB_TAIL) for r0 in range(0, ROWS_TAIL, SUB_TAIL)]
    c_za, c_ub, c_vb, c_zb, c_ga, c_gb = 0, 512, 1024, 1536, 2048, 3072
    xs = [x_ref[t, :] for t in tiles]
    hs = [_rms_h(x, ng_ref[...]) for x in xs]

    def proj(h, lo, hi):
        return jnp.dot(h, w_ref[:, lo:hi], preferred_element_type=F32)

    half = lax.broadcasted_iota(jnp.int32, (SUB_TAIL, LANES), 1) // GROUP_DIM

    def spread(stats):
        return jnp.concatenate(
            [jnp.take_along_axis(stats, 2 * p + half, axis=1) for p in range(PAIRS)], axis=1)

    vgs = [jax.nn.gelu(proj(h, c_vb, c_zb)) for h in hs]
    mus = [spread(jnp.dot(_split_bf16(vg), gm_ref[...], preferred_element_type=F32)) for vg in vgs]
    ugs = [jax.nn.gelu(proj(h, c_ub, c_vb)) for h in hs]
    ds = [vg - mu for vg, mu in zip(vgs, mus)]
    vars_ = [spread(jnp.dot((d * d).astype(BF16), gm_ref[0:D_SGU, :], preferred_element_type=F32))
             for d in ds]
    sz_bs = [jax.nn.silu(proj(h, c_zb, c_ga)) for h in hs]
    vns = [(d * lax.rsqrt(var + EPS) * lng_ref[...] + lnb_ref[...]).astype(BF16)
           for d, var in zip(ds, vars_)]
    sz_as = [jax.nn.silu(proj(h, c_za, c_ub)) for h in hs]

    pos_t = lax.broadcasted_iota(jnp.int32, (2 * SGU_CHUNK, SGU_CHUNK), 0) % SGU_CHUNK
    pos_s = lax.broadcasted_iota(jnp.int32, (2 * SGU_CHUNK, SGU_CHUNK), 1)
    causal = (pos_s // CHUNK) <= (pos_t // CHUNK)
    lane = lax.broadcasted_iota(jnp.int32, (SGU_CHUNK, LANES), 1)
    first_group = lane < GROUP_DIM
    for p in range(PAIRS):
        wsp = jnp.where(causal, ws_ref[p], 0.0).astype(BF16)
        lanes = slice(p * LANES, (p + 1) * LANES)
        for t, vn in zip(tiles, vns):
            for c in range(0, SUB_TAIL, 2 * SGU_CHUNK):
                rows = [slice(c + j * SGU_CHUNK, c + (j + 1) * SGU_CHUNK) for j in range(2)]
                r = jnp.dot(wsp, jnp.concatenate([vn[rows[0], lanes], vn[rows[1], lanes]], axis=1),
                            preferred_element_type=F32)
                for j in range(2):
                    rj = r[:, j * LANES:(j + 1) * LANES]
                    mixed_ref[t.start + rows[j].start:t.start + rows[j].stop, lanes] = jnp.where(
                        first_group, rj[0:SGU_CHUNK], rj[SGU_CHUNK:2 * SGU_CHUNK])

    p_as = [jnp.dot((o_ref[t, :] * sz_a).astype(BF16), wua_ref[...], preferred_element_type=F32)
            for t, sz_a in zip(tiles, sz_as)]
    merged = [jax.nn.sigmoid(proj(h, c_ga, c_gb)) * p_a for h, p_a in zip(hs, p_as)]

    bias = jnp.concatenate([bs_ref[...]] * (SUB_TAIL // SGU_CHUNK), axis=0)
    p_bs = [jnp.dot((ug * (mixed_ref[t, :] + bias) * sz_b).astype(BF16), wub_ref[...],
                    preferred_element_type=F32)
            for t, ug, sz_b in zip(tiles, ugs, sz_bs)]
    merged = [m + jax.nn.sigmoid(proj(h, c_gb, D_REST)) * p_b
              for m, h, p_b in zip(merged, hs, p_bs)]

    ys = [x + jnp.dot(m.astype(BF16), wo_ref[...], preferred_element_type=F32)
          for x, m in zip(xs, merged)]
    for t, y in zip(tiles, ys):
        ms = jnp.mean(y * y, axis=-1, keepdims=True)
        out_ref[t, :] = y * lax.rsqrt(ms + EPS) * fg_ref[...]


def _tail(x2, o2, norm_g, w_rest, ln_g, ln_b, ws, bs, gm2, w_up_a, w_up_b, w_out, final_g):
    n = x2.shape[0]
    const = lambda shape: pl.BlockSpec(shape, lambda i: (0,) * len(shape),
                                       pipeline_mode=pl.Buffered(1))
    return pl.pallas_call(
        _tail_kernel,
        grid=(n // ROWS_TAIL,),
        in_specs=[pl.BlockSpec((ROWS_TAIL, D_MODEL), lambda i: (i, 0)),
                  pl.BlockSpec((ROWS_TAIL, D_SB), lambda i: (i, 0)),
                  const((1, D_MODEL)),
                  const((D_MODEL, D_REST)),
                  const((1, D_SGU)), const((1, D_SGU)),
                  const((PAIRS, 2 * SGU_CHUNK, SGU_CHUNK)),
                  const((SGU_CHUNK, D_SGU)),
                  const((2 * D_SGU, LANES)),
                  const((D_SB, D_MODEL)), const((D_SGU, D_MODEL)), const((D_MODEL, D_MODEL)),
                  const((1, D_MODEL))],
        out_specs=pl.BlockSpec((ROWS_TAIL, D_MODEL), lambda i: (i, 0)),
        out_shape=jax.ShapeDtypeStruct((n, D_MODEL), F32),
        scratch_shapes=[pltpu.VMEM((ROWS_TAIL, D_SGU), F32)],
        compiler_params=pltpu.CompilerParams(
            dimension_semantics=("arbitrary",), vmem_limit_bytes=VMEM_LIMIT),
        name="tail",
    )(x2, o2, norm_g, w_rest, ln_g, ln_b, ws, bs, gm2, w_up_a, w_up_b, w_out, final_g)


def _layer(x, norm_g, w_in, ln_g, ln_b, w_s, b_s, w_up_a, w_up_b, w_out, final_g):
    b, s, _ = x.shape
    x2 = x.reshape(b * s, D_MODEL)
    q, k, v, w_rest16, w_up_a16, w_up_b16, w_out16 = _qkv_proj(
        x2, norm_g.reshape(1, D_MODEL), w_in, w_up_a, w_up_b, w_out)

    j = jnp.arange(BLK)
    tri = (j[:, None] >= j[None, :]).astype(BF16)
    tri2 = jnp.concatenate([tri, tri], axis=0)
    pair_major = lambda a: a.reshape(PAIRS, b, s, LANES)
    o = _attention(pair_major(q), pair_major(k), pair_major(v), tri2)

    gmean = (jnp.arange(D_SGU)[:, None] // GROUP_DIM == jnp.arange(LANES)[None, :] % N_GROUPS)
    gmean = (gmean.astype(F32) / GROUP_DIM).astype(BF16)
    gm2 = jnp.concatenate([gmean, gmean], axis=0)
    bias = jnp.repeat(b_s.T, GROUP_DIM, axis=1)
    return _tail(x2, o.reshape(b * s, D_SB), norm_g.reshape(1, D_MODEL), w_rest16,
                 ln_g.reshape(1, D_SGU), ln_b.reshape(1, D_SGU),
                 w_s.reshape(PAIRS, 2 * SGU_CHUNK, SGU_CHUNK), bias, gm2,
                 w_up_a16, w_up_b16, w_out16,
                 final_g.reshape(1, D_MODEL)).reshape(b, s, D_MODEL)


def kernel(x, norm_g, w_in, sgu_ln_g, sgu_ln_b, w_spatial, b_spatial, w_up_a, w_up_b, w_out,
           final_norm_g):
    assert norm_g.shape[0] == 1, "single-layer block"
    assert x.shape[1] % QROWS == 0 and (x.shape[0] * x.shape[1]) % ROWS_PROJ == 0
    return _layer(x, norm_g[0], w_in[0], sgu_ln_g[0], sgu_ln_b[0], w_spatial[0], b_spatial[0],
                  w_up_a[0], w_up_b[0], w_out[0], final_norm_g)
```

```python
import jax
import jax.numpy as jnp
from jax import lax
from jax.experimental import pallas as pl
from jax.experimental.pallas import tpu as pltpu

D_MODEL = 1024
N_HEADS = 8
HEAD_DIM = 64
D_SB = N_HEADS * HEAD_DIM
N_GROUPS = 8
GROUP_DIM = 64
D_SGU = N_GROUPS * GROUP_DIM
SGU_CHUNK = 128
CHUNK = 64
EPS = 1e-6
D_QKV = 3 * D_SB
D_REST = D_SB + 3 * D_SGU + 2 * D_MODEL

LANES = 128
BF16_SUBLANES = 16
PAIRS = D_SB // LANES

ROWS_PROJ = 1024
ROWS_TAIL = 1024
SUB_TAIL = 512
BLK = 256
QROWS = 1024
QT = 128
TILES = QROWS // QT
CHAINS = TILES * PAIRS
GROUP = 8

VMEM_LIMIT = 56 * 1024 * 1024

F32 = jnp.float32
BF16 = jnp.bfloat16


def _rms_h(x, g):
    ms = jnp.mean(x * x, axis=-1, keepdims=True)
    return (x * lax.rsqrt(ms + EPS) * g).astype(BF16)


def _split_bf16(a):
    hi = a.astype(BF16)
    lo = (a - hi.astype(F32)).astype(BF16)
    return jnp.concatenate([hi, lo], axis=-1)


def _qkv_kernel(x_ref, g_ref, wqkv_ref, wrest_ref, wua_ref, wub_ref, wo_ref,
                q_ref, k_ref, v_ref, wrest16_ref, wua16_ref, wub16_ref, wo16_ref, wqkv16_ref):
    @pl.when(pl.program_id(0) == 0)
    def _():
        wqkv16_ref[...] = wqkv_ref[...].astype(BF16)

    for src, dst in ((wrest_ref, wrest16_ref), (wua_ref, wua16_ref), (wub_ref, wub16_ref),
                     (wo_ref, wo16_ref)):
        dst[...] = src[...].astype(BF16)

    h = _rms_h(x_ref[...], g_ref[...])
    scale = HEAD_DIM ** -0.5
    for j, (ref, mul) in enumerate(((q_ref, scale), (k_ref, None), (v_ref, None))):
        y = jnp.dot(h, wqkv16_ref[:, j * D_SB:(j + 1) * D_SB], preferred_element_type=F32)
        if mul is not None:
            y = y * mul
        y = y.astype(BF16)
        for p in range(PAIRS):
            ref[p] = y[:, p * LANES:(p + 1) * LANES]


def _qkv_proj(x2, norm_g, w_in, w_up_a, w_up_b, w_out):
    n = x2.shape[0]
    steps = n // ROWS_PROJ
    cols = D_REST // steps
    rows_up, rows_out = D_SB // steps, D_MODEL // steps
    assert cols % LANES == 0 and D_QKV % cols == 0 and rows_up % BF16_SUBLANES == 0, "slab tiling"
    out = jax.ShapeDtypeStruct((PAIRS, n, LANES), BF16)
    row_spec = pl.BlockSpec((ROWS_PROJ, D_MODEL), lambda i: (i, 0))
    out_spec = pl.BlockSpec((PAIRS, ROWS_PROJ, LANES), lambda i: (0, i, 0))
    up_spec = pl.BlockSpec((rows_up, D_MODEL), lambda i: (i, 0))
    wout_spec = pl.BlockSpec((rows_out, D_MODEL), lambda i: (i, 0))
    return pl.pallas_call(
        _qkv_kernel,
        grid=(steps,),
        in_specs=[row_spec,
                  pl.BlockSpec((1, D_MODEL), lambda i: (0, 0)),
                  pl.BlockSpec((D_MODEL, D_QKV), lambda i: (0, 0), pipeline_mode=pl.Buffered(1)),
                  pl.BlockSpec((D_MODEL, cols), lambda i: (0, D_QKV // cols + i)),
                  up_spec, up_spec, wout_spec],
        out_specs=[out_spec, out_spec, out_spec,
                   pl.BlockSpec((D_MODEL, cols), lambda i: (0, i)),
                   up_spec, up_spec, wout_spec],
        out_shape=[out, out, out,
                   jax.ShapeDtypeStruct((D_MODEL, D_REST), BF16),
                   jax.ShapeDtypeStruct((D_SB, D_MODEL), BF16),
                   jax.ShapeDtypeStruct((D_SGU, D_MODEL), BF16),
                   jax.ShapeDtypeStruct((D_MODEL, D_MODEL), BF16)],
        scratch_shapes=[pltpu.VMEM((D_MODEL, D_QKV), BF16)],
        compiler_params=pltpu.CompilerParams(
            dimension_semantics=("arbitrary",), vmem_limit_bytes=VMEM_LIMIT),
        name="qkv_proj",
    )(x2, norm_g, w_in, w_in, w_up_a, w_up_b, w_out)


STICK_GONE = 88.0
MASKED = -1e30
SOFTPLUS_CLAMP = 40.0


def _softplus(z):
    return jnp.maximum(z, jnp.log(1.0 + jnp.exp(jnp.minimum(z, SOFTPLUS_CLAMP))))


def _attn_kernel(q_ref, k_ref, v_ref, tri_ref, o_ref, acc_ref, carry_ref, more_ref):
    t0 = pl.program_id(1) * QROWS
    lane = lax.broadcasted_iota(jnp.int32, (QT, LANES), 1)
    first_head = lane < HEAD_DIM
    row = lax.broadcasted_iota(jnp.int32, (2 * QT, BLK), 0) & (QT - 1)
    col = lax.broadcasted_iota(jnp.int32, (2 * QT, BLK), 1)

    def q_chain(p, r0):
        qt = q_ref[p, 0, pl.ds(r0, QT), :]
        zero = jnp.zeros_like(qt)
        return jnp.concatenate([jnp.where(first_head, qt, zero), jnp.where(first_head, zero, qt)],
                               axis=0)

    def scores(p, start, q2, visible, first_keys_visible=False):
        kb = k_ref[p, 0, pl.ds(start, BLK), :]
        z = lax.dot_general(q2, kb, (((1,), (1,)), ((), ())), preferred_element_type=F32)
        if first_keys_visible:
            return jnp.concatenate(
                [z[:, :QT], jnp.where(visible[:, QT:], z[:, QT:], MASKED)], axis=1)
        return jnp.where(visible, z, MASKED)

    def stick(z):
        csum = jnp.dot(_split_bf16(_softplus(z)), tri_ref[...], preferred_element_type=F32)
        return z - csum, csum[:, 0:1]

    def values(p, start, log_w, total, carry):
        if carry is not None:
            log_w = log_w - jnp.concatenate([carry] * (BLK // LANES), axis=1)
        vb = v_ref[p, 0, pl.ds(start, BLK), :]
        pv = jnp.dot(jnp.exp(log_w).astype(BF16), vb, preferred_element_type=F32)
        return pv, jnp.broadcast_to(total, (2 * QT, LANES))

    def window_start(tile):
        return jnp.maximum(t0 + (tile + 1) * QT - BLK, 0)

    chains = [(tile, p) for tile in range(TILES) for p in range(PAIRS)]
    starts = [pl.multiple_of(window_start(tile), QT) for tile in range(TILES)]
    zs, sticks, least = {}, {}, {}

    def stage_scores(c):
        tile, p = chains[c]
        visible = col - row < t0 + tile * QT - starts[tile]
        zs[c] = scores(p, starts[tile], q_chain(p, tile * QT), visible, tile >= 1)

    def stage_stick(c):
        sticks[c] = stick(zs[c])

    def stage_values(c):
        tile, p = chains[c]
        pv, tot = values(p, starts[tile], *sticks[c], None)
        acc_ref[c] = pv
        carry_ref[c] = tot
        least[c] = jnp.min(tot, axis=0, keepdims=True)

    stages = (stage_scores, stage_stick, stage_values)
    n_groups = CHAINS // GROUP
    for slot in range(n_groups + len(stages) - 1):
        for s, stage in enumerate(stages):
            g = slot - s
            if 0 <= g < n_groups:
                for c in range(g * GROUP, (g + 1) * GROUP):
                    stage(c)
    n_more = jnp.int32(0)
    for c, (tile, p) in enumerate(chains):
        more_ref[n_more] = jnp.int32(c)
        listed = jnp.logical_and(starts[tile] > 0, least[c][0, 0] < STICK_GONE)
        n_more = n_more + listed.astype(jnp.int32)

    def older_keys(j, _):
        c = more_ref[j]
        tile = c // PAIRS
        p = c % PAIRS
        q2 = q_chain(p, pl.multiple_of(tile * QT, QT))

        def body(state):
            end, _ = state
            start = pl.multiple_of(jnp.maximum(end - BLK, 0), QT)
            pv, tot = values(p, start, *stick(scores(p, start, q2, col < end - start)),
                             carry_ref[c])
            acc_ref[c] += pv
            carry = carry_ref[c] + tot
            carry_ref[c] = carry
            return start, jnp.logical_and(start > 0, jnp.min(carry) < STICK_GONE)

        lax.while_loop(lambda state: state[1], body, (window_start(tile), True))
        return 0

    lax.fori_loop(0, n_more, older_keys, 0)

    for tile in range(TILES):
        for p in range(PAIRS):
            a = acc_ref[tile * PAIRS + p]
            o_ref[0, tile * QT:(tile + 1) * QT, p * LANES:(p + 1) * LANES] = jnp.where(
                first_head, a[0:QT], a[QT:2 * QT])


def _attention(q, k, v, tri):
    _, b, s, _ = q.shape
    seq_spec = pl.BlockSpec((PAIRS, 1, s, LANES), lambda bi, i: (0, bi, 0, 0))
    return pl.pallas_call(
        _attn_kernel,
        grid=(b, s // QROWS),
        in_specs=[pl.BlockSpec((PAIRS, 1, QROWS, LANES), lambda bi, i: (0, bi, i, 0)),
                  seq_spec, seq_spec,
                  pl.BlockSpec((2 * BLK, BLK), lambda bi, i: (0, 0))],
        out_specs=pl.BlockSpec((1, QROWS, D_SB), lambda bi, i: (bi, i, 0)),
        out_shape=jax.ShapeDtypeStruct((b, s, D_SB), F32),
        scratch_shapes=[pltpu.VMEM((CHAINS, 2 * QT, LANES), F32),
                        pltpu.VMEM((CHAINS, 2 * QT, LANES), F32),
                        pltpu.SMEM((CHAINS,), jnp.int32)],
        compiler_params=pltpu.CompilerParams(
            dimension_semantics=("arbitrary", "arbitrary"),
            vmem_limit_bytes=VMEM_LIMIT),
        name="stickbreak",
    )(q, k, v, tri)


def _tail_kernel(x_ref, o_ref, ng_ref, w_ref, lng_ref, lnb_ref, ws_ref, bs_ref, gm_ref,
                 wua_ref, wub_ref, wo_ref, fg_ref, out_ref, mixed_ref):
    tiles = [slice(r0, r0 + SUB_TAIL) for r0 in range(0, ROWS_TAIL, SUB_TAIL)]
    c_za, c_ub, c_vb, c_zb, c_ga, c_gb = 0, 512, 1024, 1536, 2048, 3072
    xs = [x_ref[t, :] for t in tiles]
    hs = [_rms_h(x, ng_ref[...]) for x in xs]

    def proj(h, lo, hi):
        return jnp.dot(h, w_ref[:, lo:hi], preferred_element_type=F32)

    half = lax.broadcasted_iota(jnp.int32, (SUB_TAIL, LANES), 1) // GROUP_DIM

    def spread(stats):
        return jnp.concatenate(
            [jnp.take_along_axis(stats, 2 * p + half, axis=1) for p in range(PAIRS)], axis=1)

    vgs = [jax.nn.gelu(proj(h, c_vb, c_zb)) for h in hs]
    mus = [spread(jnp.dot(_split_bf16(vg), gm_ref[...], preferred_element_type=F32)) for vg in vgs]
    ugs = [jax.nn.gelu(proj(h, c_ub, c_vb)) for h in hs]
    ds = [vg - mu for vg, mu in zip(vgs, mus)]
    vars_ = [spread(jnp.dot((d * d).astype(BF16), gm_ref[0:D_SGU, :], preferred_element_type=F32))
             for d in ds]
    sz_bs = [jax.nn.silu(proj(h, c_zb, c_ga)) for h in hs]
    vns = [(d * lax.rsqrt(var + EPS) * lng_ref[...] + lnb_ref[...]).astype(BF16)
           for d, var in zip(ds, vars_)]
    sz_as = [jax.nn.silu(proj(h, c_za, c_ub)) for h in hs]

    pos_t = lax.broadcasted_iota(jnp.int32, (2 * SGU_CHUNK, SGU_CHUNK), 0) % SGU_CHUNK
    pos_s = lax.broadcasted_iota(jnp.int32, (2 * SGU_CHUNK, SGU_CHUNK), 1)
    causal = (pos_s // CHUNK) <= (pos_t // CHUNK)
    lane = lax.broadcasted_iota(jnp.int32, (SGU_CHUNK, LANES), 1)
    first_group = lane < GROUP_DIM
    for p in range(PAIRS):
        wsp = jnp.where(causal, ws_ref[p], 0.0).astype(BF16)
        lanes = slice(p * LANES, (p + 1) * LANES)
        for t, vn in zip(tiles, vns):
            for c in range(0, SUB_TAIL, 2 * SGU_CHUNK):
                rows = [slice(c + j * SGU_CHUNK, c + (j + 1) * SGU_CHUNK) for j in range(2)]
                r = jnp.dot(wsp, jnp.concatenate([vn[rows[0], lanes], vn[rows[1], lanes]], axis=1),
                            preferred_element_type=F32)
                for j in range(2):
                    rj = r[:, j * LANES:(j + 1) * LANES]
                    mixed_ref[t.start + rows[j].start:t.start + rows[j].stop, lanes] = jnp.where(
                        first_group, rj[0:SGU_CHUNK], rj[SGU_CHUNK:2 * SGU_CHUNK])

    p_as = [jnp.dot((o_ref[t, :] * sz_a).astype(BF16), wua_ref[...], preferred_element_type=F32)
            for t, sz_a in zip(tiles, sz_as)]
    merged = [jax.nn.sigmoid(proj(h, c_ga, c_gb)) * p_a for h, p_a in zip(hs, p_as)]

    bias = jnp.concatenate([bs_ref[...]] * (SUB_TAIL // SGU_CHUNK), axis=0)
    p_bs = [jnp.dot((ug * (mixed_ref[t, :] + bias) * sz_b).astype(BF16), wub_ref[...],
                    preferred_element_type=F32)
            for t, ug, sz_b in zip(tiles, ugs, sz_bs)]
    merged = [m + jax.nn.sigmoid(proj(h, c_gb, D_REST)) * p_b
              for m, h, p_b in zip(merged, hs, p_bs)]

    ys = [x + jnp.dot(m.astype(BF16), wo_ref[...], preferred_element_type=F32)
          for x, m in zip(xs, merged)]
    for t, y in zip(tiles, ys):
        ms = jnp.mean(y * y, axis=-1, keepdims=True)
        out_ref[t, :] = y * lax.rsqrt(ms + EPS) * fg_ref[...]


def _tail(x2, o2, norm_g, w_rest, ln_g, ln_b, ws, bs, gm2, w_up_a, w_up_b, w_out, final_g):
    n = x2.shape[0]
    const = lambda shape: pl.BlockSpec(shape, lambda i: (0,) * len(shape),
                                       pipeline_mode=pl.Buffered(1))
    return pl.pallas_call(
        _tail_kernel,
        grid=(n // ROWS_TAIL,),
        in_specs=[pl.BlockSpec((ROWS_TAIL, D_MODEL), lambda i: (i, 0)),
                  pl.BlockSpec((ROWS_TAIL, D_SB), lambda i: (i, 0)),
                  const((1, D_MODEL)),
                  const((D_MODEL, D_REST)),
                  const((1, D_SGU)), const((1, D_SGU)),
                  const((PAIRS, 2 * SGU_CHUNK, SGU_CHUNK)),
                  const((SGU_CHUNK, D_SGU)),
                  const((2 * D_SGU, LANES)),
                  const((D_SB, D_MODEL)), const((D_SGU, D_MODEL)), const((D_MODEL, D_MODEL)),
                  const((1, D_MODEL))],
        out_specs=pl.BlockSpec((ROWS_TAIL, D_MODEL), lambda i: (i, 0)),
        out_shape=jax.ShapeDtypeStruct((n, D_MODEL), F32),
        scratch_shapes=[pltpu.VMEM((ROWS_TAIL, D_SGU), F32)],
        compiler_params=pltpu.CompilerParams(
            dimension_semantics=("arbitrary",), vmem_limit_bytes=VMEM_LIMIT),
        name="tail",
    )(x2, o2, norm_g, w_rest, ln_g, ln_b, ws, bs, gm2, w_up_a, w_up_b, w_out, final_g)


def _layer(x, norm_g, w_in, ln_g, ln_b, w_s, b_s, w_up_a, w_up_b, w_out, final_g):
    b, s, _ = x.shape
    x2 = x.reshape(b * s, D_MODEL)
    q, k, v, w_rest16, w_up_a16, w_up_b16, w_out16 = _qkv_proj(
        x2, norm_g.reshape(1, D_MODEL), w_in, w_up_a, w_up_b, w_out)

    j = jnp.arange(BLK)
    tri = (j[:, None] >= j[None, :]).astype(BF16)
    tri2 = jnp.concatenate([tri, tri], axis=0)
    pair_major = lambda a: a.reshape(PAIRS, b, s, LANES)
    o = _attention(pair_major(q), pair_major(k), pair_major(v), tri2)

    gmean = (jnp.arange(D_SGU)[:, None] // GROUP_DIM == jnp.arange(LANES)[None, :] % N_GROUPS)
    gmean = (gmean.astype(F32) / GROUP_DIM).astype(BF16)
    gm2 = jnp.concatenate([gmean, gmean], axis=0)
    bias = jnp.repeat(b_s.T, GROUP_DIM, axis=1)
    return _tail(x2, o.reshape(b * s, D_SB), norm_g.reshape(1, D_MODEL), w_rest16,
                 ln_g.reshape(1, D_SGU), ln_b.reshape(1, D_SGU),
                 w_s.reshape(PAIRS, 2 * SGU_CHUNK, SGU_CHUNK), bias, gm2,
                 w_up_a16, w_up_b16, w_out16,
                 final_g.reshape(1, D_MODEL)).reshape(b, s, D_MODEL)


def kernel(x, norm_g, w_in, sgu_ln_g, sgu_ln_b, w_spatial, b_spatial, w_up_a, w_up_b, w_out,
           final_norm_g):
    assert norm_g.shape[0] == 1, "single-layer block"
    assert x.shape[1] % QROWS == 0 and (x.shape[0] * x.shape[1]) % ROWS_PROJ == 0
    return _layer(x, norm_g[0], w_in[0], sgu_ln_g[0], sgu_ln_b[0], w_spatial[0], b_spatial[0],
                  w_up_a[0], w_up_b[0], w_out[0], final_norm_g)
```

```python
import jax
import jax.numpy as jnp
from jax import lax
from jax.experimental import pallas as pl
from jax.experimental.pallas import tpu as pltpu

D_MODEL = 1024
N_HEADS = 8
HEAD_DIM = 64
D_SB = N_HEADS * HEAD_DIM
N_GROUPS = 8
GROUP_DIM = 64
D_SGU = N_GROUPS * GROUP_DIM
SGU_CHUNK = 128
CHUNK = 64
EPS = 1e-6
D_QKV = 3 * D_SB
D_REST = D_SB + 3 * D_SGU + 2 * D_MODEL

LANES = 128
BF16_SUBLANES = 16
PAIRS = D_SB // LANES

ROWS_PROJ = 1024
ROWS_TAIL = 1024
SUB_TAIL = 512
BLK = 256
QROWS = 1024
QT = 128
TILES = QROWS // QT
CHAINS = TILES * PAIRS
GROUP = 8

VMEM_LIMIT = 56 * 1024 * 1024

F32 = jnp.float32
BF16 = jnp.bfloat16


def _rms_h(x, g):
    ms = jnp.mean(x * x, axis=-1, keepdims=True)
    return (x * lax.rsqrt(ms + EPS) * g).astype(BF16)


def _split_bf16(a):
    hi = a.astype(BF16)
    lo = (a - hi.astype(F32)).astype(BF16)
    return jnp.concatenate([hi, lo], axis=-1)


def _qkv_kernel(x_ref, g_ref, w_ref, q_ref, k_ref, v_ref, w16_ref):
    @pl.when(pl.program_id(0) == 0)
    def _():
        w16_ref[...] = w_ref[...].astype(BF16)

    h = _rms_h(x_ref[...], g_ref[...])
    scale = HEAD_DIM ** -0.5
    for j, (ref, mul) in enumerate(((q_ref, scale), (k_ref, None), (v_ref, None))):
        y = jnp.dot(h, w16_ref[:, j * D_SB:(j + 1) * D_SB], preferred_element_type=F32)
        if mul is not None:
            y = y * mul
        y = y.astype(BF16)
        for p in range(PAIRS):
            ref[p] = y[:, p * LANES:(p + 1) * LANES]


def _qkv_proj(x2, norm_g, w_in):
    n = x2.shape[0]
    out = jax.ShapeDtypeStruct((PAIRS, n, LANES), BF16)
    row_spec = pl.BlockSpec((ROWS_PROJ, D_MODEL), lambda i: (i, 0))
    out_spec = pl.BlockSpec((PAIRS, ROWS_PROJ, LANES), lambda i: (0, i, 0))
    return pl.pallas_call(
        _qkv_kernel,
        grid=(n // ROWS_PROJ,),
        in_specs=[row_spec,
                  pl.BlockSpec((1, D_MODEL), lambda i: (0, 0)),
                  pl.BlockSpec((D_MODEL, D_QKV), lambda i: (0, 0), pipeline_mode=pl.Buffered(1))],
        out_specs=[out_spec, out_spec, out_spec],
        out_shape=[out, out, out],
        scratch_shapes=[pltpu.VMEM((D_MODEL, D_QKV), BF16)],
        compiler_params=pltpu.CompilerParams(
            dimension_semantics=("arbitrary",), vmem_limit_bytes=VMEM_LIMIT),
        name="qkv_proj",
    )(x2, norm_g, w_in)


def _weight_slab_specs(steps, step_of):
    cols = D_REST // steps
    rows_up, rows_out = D_SB // steps, D_MODEL // steps
    assert cols % LANES == 0 and D_QKV % cols == 0 and rows_up % BF16_SUBLANES == 0, "slab tiling"
    rest_in = pl.BlockSpec((D_MODEL, cols), lambda *g: (0, D_QKV // cols + step_of(*g)))
    rest_out = pl.BlockSpec((D_MODEL, cols), lambda *g: (0, step_of(*g)))
    up = pl.BlockSpec((rows_up, D_MODEL), lambda *g: (step_of(*g), 0))
    out = pl.BlockSpec((rows_out, D_MODEL), lambda *g: (step_of(*g), 0))
    shapes = [jax.ShapeDtypeStruct((D_MODEL, D_REST), BF16),
              jax.ShapeDtypeStruct((D_SB, D_MODEL), BF16),
              jax.ShapeDtypeStruct((D_SGU, D_MODEL), BF16),
              jax.ShapeDtypeStruct((D_MODEL, D_MODEL), BF16)]
    return [rest_in, up, up, out], [rest_out, up, up, out], shapes


STICK_GONE = 88.0
MASKED = -1e30
SOFTPLUS_CLAMP = 40.0


def _softplus(z):
    return jnp.maximum(z, jnp.log(1.0 + jnp.exp(jnp.minimum(z, SOFTPLUS_CLAMP))))


def _attn_kernel(q_ref, k_ref, v_ref, tri_ref, wrest_ref, wua_ref, wub_ref, wo_ref,
                 o_ref, wrest16_ref, wua16_ref, wub16_ref, wo16_ref, acc_ref, carry_ref, more_ref):
    for src, dst in ((wrest_ref, wrest16_ref), (wua_ref, wua16_ref), (wub_ref, wub16_ref),
                     (wo_ref, wo16_ref)):
        dst[...] = src[...].astype(BF16)

    t0 = pl.program_id(1) * QROWS
    lane = lax.broadcasted_iota(jnp.int32, (QT, LANES), 1)
    first_head = lane < HEAD_DIM
    row = lax.broadcasted_iota(jnp.int32, (2 * QT, BLK), 0) & (QT - 1)
    col = lax.broadcasted_iota(jnp.int32, (2 * QT, BLK), 1)

    def q_chain(p, r0):
        qt = q_ref[p, 0, pl.ds(r0, QT), :]
        zero = jnp.zeros_like(qt)
        return jnp.concatenate([jnp.where(first_head, qt, zero), jnp.where(first_head, zero, qt)],
                               axis=0)

    def scores(p, start, q2, visible, first_keys_visible=False):
        kb = k_ref[p, 0, pl.ds(start, BLK), :]
        z = lax.dot_general(q2, kb, (((1,), (1,)), ((), ())), preferred_element_type=F32)
        if first_keys_visible:
            return jnp.concatenate(
                [z[:, :QT], jnp.where(visible[:, QT:], z[:, QT:], MASKED)], axis=1)
        return jnp.where(visible, z, MASKED)

    def stick(z):
        csum = jnp.dot(_split_bf16(_softplus(z)), tri_ref[...], preferred_element_type=F32)
        return z - csum, csum[:, 0:1]

    def values(p, start, log_w, total, carry):
        if carry is not None:
            log_w = log_w - jnp.concatenate([carry] * (BLK // LANES), axis=1)
        vb = v_ref[p, 0, pl.ds(start, BLK), :]
        pv = jnp.dot(jnp.exp(log_w).astype(BF16), vb, preferred_element_type=F32)
        return pv, jnp.broadcast_to(total, (2 * QT, LANES))

    def window_start(tile):
        return jnp.maximum(t0 + (tile + 1) * QT - BLK, 0)

    chains = [(tile, p) for tile in range(TILES) for p in range(PAIRS)]
    starts = [pl.multiple_of(window_start(tile), QT) for tile in range(TILES)]
    zs, sticks, least = {}, {}, {}

    def stage_scores(c):
        tile, p = chains[c]
        visible = col - row < t0 + tile * QT - starts[tile]
        zs[c] = scores(p, starts[tile], q_chain(p, tile * QT), visible, tile >= 1)

    def stage_stick(c):
        sticks[c] = stick(zs[c])

    def stage_values(c):
        tile, p = chains[c]
        pv, tot = values(p, starts[tile], *sticks[c], None)
        acc_ref[c] = pv
        carry_ref[c] = tot
        least[c] = jnp.min(tot, axis=0, keepdims=True)

    stages = (stage_scores, stage_stick, stage_values)
    n_groups = CHAINS // GROUP
    for slot in range(n_groups + len(stages) - 1):
        for s, stage in enumerate(stages):
            g = slot - s
            if 0 <= g < n_groups:
                for c in range(g * GROUP, (g + 1) * GROUP):
                    stage(c)
    n_more = jnp.int32(0)
    for c, (tile, p) in enumerate(chains):
        more_ref[n_more] = jnp.int32(c)
        listed = jnp.logical_and(starts[tile] > 0, least[c][0, 0] < STICK_GONE)
        n_more = n_more + listed.astype(jnp.int32)

    def older_keys(j, _):
        c = more_ref[j]
        tile = c // PAIRS
        p = c % PAIRS
        q2 = q_chain(p, pl.multiple_of(tile * QT, QT))

        def body(state):
            end, _ = state
            start = pl.multiple_of(jnp.maximum(end - BLK, 0), QT)
            pv, tot = values(p, start, *stick(scores(p, start, q2, col < end - start)),
                             carry_ref[c])
            acc_ref[c] += pv
            carry = carry_ref[c] + tot
            carry_ref[c] = carry
            return start, jnp.logical_and(start > 0, jnp.min(carry) < STICK_GONE)

        lax.while_loop(lambda state: state[1], body, (window_start(tile), True))
        return 0

    lax.fori_loop(0, n_more, older_keys, 0)

    for tile in range(TILES):
        for p in range(PAIRS):
            a = acc_ref[tile * PAIRS + p]
            o_ref[0, tile * QT:(tile + 1) * QT, p * LANES:(p + 1) * LANES] = jnp.where(
                first_head, a[0:QT], a[QT:2 * QT])


def _attention(q, k, v, tri, w_in, w_up_a, w_up_b, w_out):
    _, b, s, _ = q.shape
    steps_per_batch = s // QROWS
    slab_in, slab_out, slab_shapes = _weight_slab_specs(
        b * steps_per_batch, lambda bi, i: bi * steps_per_batch + i)
    seq_spec = pl.BlockSpec((PAIRS, 1, s, LANES), lambda bi, i: (0, bi, 0, 0))
    return pl.pallas_call(
        _attn_kernel,
        grid=(b, s // QROWS),
        in_specs=[pl.BlockSpec((PAIRS, 1, QROWS, LANES), lambda bi, i: (0, bi, i, 0)),
                  seq_spec, seq_spec,
                  pl.BlockSpec((2 * BLK, BLK), lambda bi, i: (0, 0))] + slab_in,
        out_specs=[pl.BlockSpec((1, QROWS, D_SB), lambda bi, i: (bi, i, 0))] + slab_out,
        out_shape=[jax.ShapeDtypeStruct((b, s, D_SB), F32)] + slab_shapes,
        scratch_shapes=[pltpu.VMEM((CHAINS, 2 * QT, LANES), F32),
                        pltpu.VMEM((CHAINS, 2 * QT, LANES), F32),
                        pltpu.SMEM((CHAINS,), jnp.int32)],
        compiler_params=pltpu.CompilerParams(
            dimension_semantics=("arbitrary", "arbitrary"),
            vmem_limit_bytes=VMEM_LIMIT),
        name="stickbreak",
    )(q, k, v, tri, w_in, w_up_a, w_up_b, w_out)


def _tail_kernel(x_ref, o_ref, ng_ref, w_ref, lng_ref, lnb_ref, ws_ref, bs_ref, gm_ref,
                 wua_ref, wub_ref, wo_ref, fg_ref, out_ref, mixed_ref):
    tiles = [slice(r0, r0 + SUB_TAIL) for r0 in range(0, ROWS_TAIL, SUB_TAIL)]
    c_za, c_ub, c_vb, c_zb, c_ga, c_gb = 0, 512, 1024, 1536, 2048, 3072
    xs = [x_ref[t, :] for t in tiles]
    hs = [_rms_h(x, ng_ref[...]) for x in xs]

    def proj(h, lo, hi):
        return jnp.dot(h, w_ref[:, lo:hi], preferred_element_type=F32)

    half = lax.broadcasted_iota(jnp.int32, (SUB_TAIL, LANES), 1) // GROUP_DIM

    def spread(stats):
        return jnp.concatenate(
            [jnp.take_along_axis(stats, 2 * p + half, axis=1) for p in range(PAIRS)], axis=1)

    vgs = [jax.nn.gelu(proj(h, c_vb, c_zb)) for h in hs]
    mus = [spread(jnp.dot(_split_bf16(vg), gm_ref[...], preferred_element_type=F32)) for vg in vgs]
    ugs = [jax.nn.gelu(proj(h, c_ub, c_vb)) for h in hs]
    ds = [vg - mu for vg, mu in zip(vgs, mus)]
    vars_ = [spread(jnp.dot((d * d).astype(BF16), gm_ref[0:D_SGU, :], preferred_element_type=F32))
             for d in ds]
    sz_bs = [jax.nn.silu(proj(h, c_zb, c_ga)) for h in hs]
    vns = [(d * lax.rsqrt(var + EPS) * lng_ref[...] + lnb_ref[...]).astype(BF16)
           for d, var in zip(ds, vars_)]
    sz_as = [jax.nn.silu(proj(h, c_za, c_ub)) for h in hs]

    pos_t = lax.broadcasted_iota(jnp.int32, (2 * SGU_CHUNK, SGU_CHUNK), 0) % SGU_CHUNK
    pos_s = lax.broadcasted_iota(jnp.int32, (2 * SGU_CHUNK, SGU_CHUNK), 1)
    causal = (pos_s // CHUNK) <= (pos_t // CHUNK)
    lane = lax.broadcasted_iota(jnp.int32, (SGU_CHUNK, LANES), 1)
    first_group = lane < GROUP_DIM
    for p in range(PAIRS):
        wsp = jnp.where(causal, ws_ref[p], 0.0).astype(BF16)
        lanes = slice(p * LANES, (p + 1) * LANES)
        for t, vn in zip(tiles, vns):
            for c in range(0, SUB_TAIL, 2 * SGU_CHUNK):
                rows = [slice(c + j * SGU_CHUNK, c + (j + 1) * SGU_CHUNK) for j in range(2)]
                r = jnp.dot(wsp, jnp.concatenate([vn[rows[0], lanes], vn[rows[1], lanes]], axis=1),
                            preferred_element_type=F32)
                for j in range(2):
                    rj = r[:, j * LANES:(j + 1) * LANES]
                    mixed_ref[t.start + rows[j].start:t.start + rows[j].stop, lanes] = jnp.where(
                        first_group, rj[0:SGU_CHUNK], rj[SGU_CHUNK:2 * SGU_CHUNK])

    p_as = [jnp.dot((o_ref[t, :] * sz_a).astype(BF16), wua_ref[...], preferred_element_type=F32)
            for t, sz_a in zip(tiles, sz_as)]
    merged = [jax.nn.sigmoid(proj(h, c_ga, c_gb)) * p_a for h, p_a in zip(hs, p_as)]

    bias = jnp.concatenate([bs_ref[...]] * (SUB_TAIL // SGU_CHUNK), axis=0)
    p_bs = [jnp.dot((ug * (mixed_ref[t, :] + bias) * sz_b).astype(BF16), wub_ref[...],
                    preferred_element_type=F32)
            for t, ug, sz_b in zip(tiles, ugs, sz_bs)]
    merged = [m + jax.nn.sigmoid(proj(h, c_gb, D_REST)) * p_b
              for m, h, p_b in zip(merged, hs, p_bs)]

    ys = [x + jnp.dot(m.astype(BF16), wo_ref[...], preferred_element_type=F32)
          for x, m in zip(xs, merged)]
    for t, y in zip(tiles, ys):
        ms = jnp.mean(y * y, axis=-1, keepdims=True)
        out_ref[t, :] = y * lax.rsqrt(ms + EPS) * fg_ref[...]


def _tail(x2, o2, norm_g, w_rest, ln_g, ln_b, ws, bs, gm2, w_up_a, w_up_b, w_out, final_g):
    n = x2.shape[0]
    const = lambda shape: pl.BlockSpec(shape, lambda i: (0,) * len(shape),
                                       pipeline_mode=pl.Buffered(1))
    return pl.pallas_call(
        _tail_kernel,
        grid=(n // ROWS_TAIL,),
        in_specs=[pl.BlockSpec((ROWS_TAIL, D_MODEL), lambda i: (i, 0)),
                  pl.BlockSpec((ROWS_TAIL, D_SB), lambda i: (i, 0)),
                  const((1, D_MODEL)),
                  const((D_MODEL, D_REST)),
                  const((1, D_SGU)), const((1, D_SGU)),
                  const((PAIRS, 2 * SGU_CHUNK, SGU_CHUNK)),
                  const((SGU_CHUNK, D_SGU)),
                  const((2 * D_SGU, LANES)),
                  const((D_SB, D_MODEL)), const((D_SGU, D_MODEL)), const((D_MODEL, D_MODEL)),
                  const((1, D_MODEL))],
        out_specs=pl.BlockSpec((ROWS_TAIL, D_MODEL), lambda i: (i, 0)),
        out_shape=jax.ShapeDtypeStruct((n, D_MODEL), F32),
        scratch_shapes=[pltpu.VMEM((ROWS_TAIL, D_SGU), F32)],
        compiler_params=pltpu.CompilerParams(
            dimension_semantics=("arbitrary",), vmem_limit_bytes=VMEM_LIMIT),
        name="tail",
    )(x2, o2, norm_g, w_rest, ln_g, ln_b, ws, bs, gm2, w_up_a, w_up_b, w_out, final_g)


def _layer(x, norm_g, w_in, ln_g, ln_b, w_s, b_s, w_up_a, w_up_b, w_out, final_g):
    b, s, _ = x.shape
    x2 = x.reshape(b * s, D_MODEL)
    q, k, v = _qkv_proj(x2, norm_g.reshape(1, D_MODEL), w_in)

    j = jnp.arange(BLK)
    tri = (j[:, None] >= j[None, :]).astype(BF16)
    tri2 = jnp.concatenate([tri, tri], axis=0)
    pair_major = lambda a: a.reshape(PAIRS, b, s, LANES)
    o, w_rest16, w_up_a16, w_up_b16, w_out16 = _attention(
        pair_major(q), pair_major(k), pair_major(v), tri2, w_in, w_up_a, w_up_b, w_out)

    gmean = (jnp.arange(D_SGU)[:, None] // GROUP_DIM == jnp.arange(LANES)[None, :] % N_GROUPS)
    gmean = (gmean.astype(F32) / GROUP_DIM).astype(BF16)
    gm2 = jnp.concatenate([gmean, gmean], axis=0)
    bias = jnp.repeat(b_s.T, GROUP_DIM, axis=1)
    return _tail(x2, o.reshape(b * s, D_SB), norm_g.reshape(1, D_MODEL), w_rest16,
                 ln_g.reshape(1, D_SGU), ln_b.reshape(1, D_SGU),
                 w_s.reshape(PAIRS, 2 * SGU_CHUNK, SGU_CHUNK), bias, gm2,
                 w_up_a16, w_up_b16, w_out16,
                 final_g.reshape(1, D_MODEL)).reshape(b, s, D_MODEL)


def kernel(x, norm_g, w_in, sgu_ln_g, sgu_ln_b, w_spatial, b_spatial, w_up_a, w_up_b, w_out,
           final_norm_g):
    assert norm_g.shape[0] == 1, "single-layer block"
    assert x.shape[1] % QROWS == 0 and (x.shape[0] * x.shape[1]) % ROWS_PROJ == 0
    return _layer(x, norm_g[0], w_in[0], sgu_ln_g[0], sgu_ln_b[0], w_spatial[0], b_spatial[0],
                  w_up_a[0], w_up_b[0], w_out[0], final_norm_g)
```

```python
import jax
import jax.numpy as jnp
from jax import lax
from jax.experimental import pallas as pl
from jax.experimental.pallas import tpu as pltpu

D_MODEL = 1024
N_HEADS = 8
HEAD_DIM = 64
D_SB = N_HEADS * HEAD_DIM
N_GROUPS = 8
GROUP_DIM = 64
D_SGU = N_GROUPS * GROUP_DIM
SGU_CHUNK = 128
CHUNK = 64
EPS = 1e-6
D_QKV = 3 * D_SB
D_REST = D_SB + 3 * D_SGU + 2 * D_MODEL

LANES = 128
BF16_SUBLANES = 16
PAIRS = D_SB // LANES

ROWS_PROJ = 1024
ROWS_TAIL = 1024
SUB_TAIL = 512
BLK = 256
QROWS = 1024
QT = 128
TILES = QROWS // QT
CHAINS = TILES * PAIRS
GROUP = 8

VMEM_LIMIT = 56 * 1024 * 1024

F32 = jnp.float32
BF16 = jnp.bfloat16


def _rms_h(x, g):
    ms = jnp.mean(x * x, axis=-1, keepdims=True)
    return (x * lax.rsqrt(ms + EPS) * g).astype(BF16)


def _split_bf16(a):
    hi = a.astype(BF16)
    lo = (a - hi.astype(F32)).astype(BF16)
    return jnp.concatenate([hi, lo], axis=-1)


def _qkv_kernel(x_ref, g_ref, w_ref, q_ref, k_ref, v_ref, w16_ref):
    @pl.when(pl.program_id(0) == 0)
    def _():
        w16_ref[...] = w_ref[...].astype(BF16)

    scale = HEAD_DIM ** -0.5
    for r0 in range(0, ROWS_PROJ, ROWS_PROJ // 2):
        rows = slice(r0, r0 + ROWS_PROJ // 2)
        h = _rms_h(x_ref[rows, :], g_ref[...])
        for j, (ref, mul) in enumerate(((q_ref, scale), (k_ref, None), (v_ref, None))):
            y = jnp.dot(h, w16_ref[:, j * D_SB:(j + 1) * D_SB], preferred_element_type=F32)
            if mul is not None:
                y = y * mul
            y = y.astype(BF16)
            for p in range(PAIRS):
                ref[p, rows, :] = y[:, p * LANES:(p + 1) * LANES]


def _qkv_proj(x2, norm_g, w_in):
    n = x2.shape[0]
    out = jax.ShapeDtypeStruct((PAIRS, n, LANES), BF16)
    row_spec = pl.BlockSpec((ROWS_PROJ, D_MODEL), lambda i: (i, 0))
    out_spec = pl.BlockSpec((PAIRS, ROWS_PROJ, LANES), lambda i: (0, i, 0))
    return pl.pallas_call(
        _qkv_kernel,
        grid=(n // ROWS_PROJ,),
        in_specs=[row_spec,
                  pl.BlockSpec((1, D_MODEL), lambda i: (0, 0)),
                  pl.BlockSpec((D_MODEL, D_QKV), lambda i: (0, 0), pipeline_mode=pl.Buffered(1))],
        out_specs=[out_spec, out_spec, out_spec],
        out_shape=[out, out, out],
        scratch_shapes=[pltpu.VMEM((D_MODEL, D_QKV), BF16)],
        compiler_params=pltpu.CompilerParams(
            dimension_semantics=("arbitrary",), vmem_limit_bytes=VMEM_LIMIT),
        name="qkv_proj",
    )(x2, norm_g, w_in)


def _weight_slab_specs(steps, step_of):
    cols = D_REST // steps
    rows_up, rows_out = D_SB // steps, D_MODEL // steps
    assert cols % LANES == 0 and D_QKV % cols == 0 and rows_up % BF16_SUBLANES == 0, "slab tiling"
    rest_in = pl.BlockSpec((D_MODEL, cols), lambda *g: (0, D_QKV // cols + step_of(*g)))
    rest_out = pl.BlockSpec((D_MODEL, cols), lambda *g: (0, step_of(*g)))
    up = pl.BlockSpec((rows_up, D_MODEL), lambda *g: (step_of(*g), 0))
    out = pl.BlockSpec((rows_out, D_MODEL), lambda *g: (step_of(*g), 0))
    shapes = [jax.ShapeDtypeStruct((D_MODEL, D_REST), BF16),
              jax.ShapeDtypeStruct((D_SB, D_MODEL), BF16),
              jax.ShapeDtypeStruct((D_SGU, D_MODEL), BF16),
              jax.ShapeDtypeStruct((D_MODEL, D_MODEL), BF16)]
    return [rest_in, up, up, out], [rest_out, up, up, out], shapes


STICK_GONE = 88.0
MASKED = -1e30
SOFTPLUS_CLAMP = 40.0


def _softplus(z):
    return jnp.maximum(z, jnp.log(1.0 + jnp.exp(jnp.minimum(z, SOFTPLUS_CLAMP))))


def _attn_kernel(q_ref, k_ref, v_ref, tri_ref, wrest_ref, wua_ref, wub_ref, wo_ref,
                 o_ref, wrest16_ref, wua16_ref, wub16_ref, wo16_ref, acc_ref, carry_ref, more_ref):
    for src, dst in ((wrest_ref, wrest16_ref), (wua_ref, wua16_ref), (wub_ref, wub16_ref),
                     (wo_ref, wo16_ref)):
        dst[...] = src[...].astype(BF16)

    t0 = pl.program_id(1) * QROWS
    lane = lax.broadcasted_iota(jnp.int32, (QT, LANES), 1)
    first_head = lane < HEAD_DIM
    row = lax.broadcasted_iota(jnp.int32, (2 * QT, BLK), 0) & (QT - 1)
    col = lax.broadcasted_iota(jnp.int32, (2 * QT, BLK), 1)

    def q_chain(p, r0):
        qt = q_ref[p, 0, pl.ds(r0, QT), :]
        zero = jnp.zeros_like(qt)
        return jnp.concatenate([jnp.where(first_head, qt, zero), jnp.where(first_head, zero, qt)],
                               axis=0)

    def scores(p, start, q2, visible, first_keys_visible=False):
        kb = k_ref[p, 0, pl.ds(start, BLK), :]
        z = lax.dot_general(q2, kb, (((1,), (1,)), ((), ())), preferred_element_type=F32)
        if first_keys_visible:
            return jnp.concatenate(
                [z[:, :QT], jnp.where(visible[:, QT:], z[:, QT:], MASKED)], axis=1)
        return jnp.where(visible, z, MASKED)

    def stick(z):
        csum = jnp.dot(_split_bf16(_softplus(z)), tri_ref[...], preferred_element_type=F32)
        return z - csum, csum[:, 0:1]

    def values(p, start, log_w, total, carry):
        if carry is not None:
            log_w = log_w - jnp.concatenate([carry] * (BLK // LANES), axis=1)
        vb = v_ref[p, 0, pl.ds(start, BLK), :]
        pv = jnp.dot(jnp.exp(log_w).astype(BF16), vb, preferred_element_type=F32)
        return pv, jnp.broadcast_to(total, (2 * QT, LANES))

    def window_start(tile):
        return jnp.maximum(t0 + (tile + 1) * QT - BLK, 0)

    chains = [(tile, p) for tile in range(TILES) for p in range(PAIRS)]
    starts = [pl.multiple_of(window_start(tile), QT) for tile in range(TILES)]
    zs, sticks, least = {}, {}, {}

    def stage_scores(c):
        tile, p = chains[c]
        visible = col - row < t0 + tile * QT - starts[tile]
        zs[c] = scores(p, starts[tile], q_chain(p, tile * QT), visible, tile >= 1)

    def stage_stick(c):
        sticks[c] = stick(zs[c])

    def stage_values(c):
        tile, p = chains[c]
        pv, tot = values(p, starts[tile], *sticks[c], None)
        acc_ref[c] = pv
        carry_ref[c] = tot
        least[c] = jnp.min(tot, axis=0, keepdims=True)

    stages = (stage_scores, stage_stick, stage_values)
    n_groups = CHAINS // GROUP
    for slot in range(n_groups + len(stages) - 1):
        for s, stage in enumerate(stages):
            g = slot - s
            if 0 <= g < n_groups:
                for c in range(g * GROUP, (g + 1) * GROUP):
                    stage(c)
    n_more = jnp.int32(0)
    for c, (tile, p) in enumerate(chains):
        more_ref[n_more] = jnp.int32(c)
        listed = jnp.logical_and(starts[tile] > 0, least[c][0, 0] < STICK_GONE)
        n_more = n_more + listed.astype(jnp.int32)

    def older_keys(j, _):
        c = more_ref[j]
        tile = c // PAIRS
        p = c % PAIRS
        q2 = q_chain(p, pl.multiple_of(tile * QT, QT))

        def body(state):
            end, _ = state
            start = pl.multiple_of(jnp.maximum(end - BLK, 0), QT)
            pv, tot = values(p, start, *stick(scores(p, start, q2, col < end - start)),
                             carry_ref[c])
            acc_ref[c] += pv
            carry = carry_ref[c] + tot
            carry_ref[c] = carry
            return start, jnp.logical_and(start > 0, jnp.min(carry) < STICK_GONE)

        lax.while_loop(lambda state: state[1], body, (window_start(tile), True))
        return 0

    lax.fori_loop(0, n_more, older_keys, 0)

    for tile in range(TILES):
        for p in range(PAIRS):
            a = acc_ref[tile * PAIRS + p]
            o_ref[0, tile * QT:(tile + 1) * QT, p * LANES:(p + 1) * LANES] = jnp.where(
                first_head, a[0:QT], a[QT:2 * QT])


def _attention(q, k, v, tri, w_in, w_up_a, w_up_b, w_out):
    _, b, s, _ = q.shape
    steps_per_batch = s // QROWS
    slab_in, slab_out, slab_shapes = _weight_slab_specs(
        b * steps_per_batch, lambda bi, i: bi * steps_per_batch + i)
    seq_spec = pl.BlockSpec((PAIRS, 1, s, LANES), lambda bi, i: (0, bi, 0, 0))
    return pl.pallas_call(
        _attn_kernel,
        grid=(b, s // QROWS),
        in_specs=[pl.BlockSpec((PAIRS, 1, QROWS, LANES), lambda bi, i: (0, bi, i, 0)),
                  seq_spec, seq_spec,
                  pl.BlockSpec((2 * BLK, BLK), lambda bi, i: (0, 0))] + slab_in,
        out_specs=[pl.BlockSpec((1, QROWS, D_SB), lambda bi, i: (bi, i, 0))] + slab_out,
        out_shape=[jax.ShapeDtypeStruct((b, s, D_SB), F32)] + slab_shapes,
        scratch_shapes=[pltpu.VMEM((CHAINS, 2 * QT, LANES), F32),
                        pltpu.VMEM((CHAINS, 2 * QT, LANES), F32),
                        pltpu.SMEM((CHAINS,), jnp.int32)],
        compiler_params=pltpu.CompilerParams(
            dimension_semantics=("arbitrary", "arbitrary"),
            vmem_limit_bytes=VMEM_LIMIT),
        name="stickbreak",
    )(q, k, v, tri, w_in, w_up_a, w_up_b, w_out)


def _tail_kernel(x_ref, o_ref, ng_ref, w_ref, lng_ref, lnb_ref, ws_ref, bs_ref, gm_ref,
                 wua_ref, wub_ref, wo_ref, fg_ref, out_ref, mixed_ref):
    tiles = [slice(r0, r0 + SUB_TAIL) for r0 in range(0, ROWS_TAIL, SUB_TAIL)]
    c_za, c_ub, c_vb, c_zb, c_ga, c_gb = 0, 512, 1024, 1536, 2048, 3072
    xs = [x_ref[t, :] for t in tiles]
    hs = [_rms_h(x, ng_ref[...]) for x in xs]

    def proj(h, lo, hi):
        return jnp.dot(h, w_ref[:, lo:hi], preferred_element_type=F32)

    half = lax.broadcasted_iota(jnp.int32, (SUB_TAIL, LANES), 1) // GROUP_DIM

    def spread(stats):
        return jnp.concatenate(
            [jnp.take_along_axis(stats, 2 * p + half, axis=1) for p in range(PAIRS)], axis=1)

    vgs = [jax.nn.gelu(proj(h, c_vb, c_zb)) for h in hs]
    mus = [spread(jnp.dot(_split_bf16(vg), gm_ref[...], preferred_element_type=F32)) for vg in vgs]
    ugs = [jax.nn.gelu(proj(h, c_ub, c_vb)) for h in hs]
    ds = [vg - mu for vg, mu in zip(vgs, mus)]
    vars_ = [spread(jnp.dot((d * d).astype(BF16), gm_ref[0:D_SGU, :], preferred_element_type=F32))
             for d in ds]
    sz_bs = [jax.nn.silu(proj(h, c_zb, c_ga)) for h in hs]
    vns = [(d * lax.rsqrt(var + EPS) * lng_ref[...] + lnb_ref[...]).astype(BF16)
           for d, var in zip(ds, vars_)]
    sz_as = [jax.nn.silu(proj(h, c_za, c_ub)) for h in hs]

    pos_t = lax.broadcasted_iota(jnp.int32, (2 * SGU_CHUNK, SGU_CHUNK), 0) % SGU_CHUNK
    pos_s = lax.broadcasted_iota(jnp.int32, (2 * SGU_CHUNK, SGU_CHUNK), 1)
    causal = (pos_s // CHUNK) <= (pos_t // CHUNK)
    lane = lax.broadcasted_iota(jnp.int32, (SGU_CHUNK, LANES), 1)
    first_group = lane < GROUP_DIM
    for p in range(PAIRS):
        wsp = jnp.where(causal, ws_ref[p], 0.0).astype(BF16)
        lanes = slice(p * LANES, (p + 1) * LANES)
        for t, vn in zip(tiles, vns):
            for c in range(0, SUB_TAIL, 2 * SGU_CHUNK):
                rows = [slice(c + j * SGU_CHUNK, c + (j + 1) * SGU_CHUNK) for j in range(2)]
                r = jnp.dot(wsp, jnp.concatenate([vn[rows[0], lanes], vn[rows[1], lanes]], axis=1),
                            preferred_element_type=F32)
                for j in range(2):
                    rj = r[:, j * LANES:(j + 1) * LANES]
                    mixed_ref[t.start + rows[j].start:t.start + rows[j].stop, lanes] = jnp.where(
                        first_group, rj[0:SGU_CHUNK], rj[SGU_CHUNK:2 * SGU_CHUNK])

    p_as = [jnp.dot((o_ref[t, :] * sz_a).astype(BF16), wua_ref[...], preferred_element_type=F32)
            for t, sz_a in zip(tiles, sz_as)]
    merged = [jax.nn.sigmoid(proj(h, c_ga, c_gb)) * p_a for h, p_a in zip(hs, p_as)]

    bias = jnp.concatenate([bs_ref[...]] * (SUB_TAIL // SGU_CHUNK), axis=0)
    p_bs = [jnp.dot((ug * (mixed_ref[t, :] + bias) * sz_b).astype(BF16), wub_ref[...],
                    preferred_element_type=F32)
            for t, ug, sz_b in zip(tiles, ugs, sz_bs)]
    merged = [m + jax.nn.sigmoid(proj(h, c_gb, D_REST)) * p_b
              for m, h, p_b in zip(merged, hs, p_bs)]

    ys = [x + jnp.dot(m.astype(BF16), wo_ref[...], preferred_element_type=F32)
          for x, m in zip(xs, merged)]
    for t, y in zip(tiles, ys):
        ms = jnp.mean(y * y, axis=-1, keepdims=True)
        out_ref[t, :] = y * lax.rsqrt(ms + EPS) * fg_ref[...]


def _tail(x2, o2, norm_g, w_rest, ln_g, ln_b, ws, bs, gm2, w_up_a, w_up_b, w_out, final_g):
    n = x2.shape[0]
    const = lambda shape: pl.BlockSpec(shape, lambda i: (0,) * len(shape),
                                       pipeline_mode=pl.Buffered(1))
    return pl.pallas_call(
        _tail_kernel,
        grid=(n // ROWS_TAIL,),
        in_specs=[pl.BlockSpec((ROWS_TAIL, D_MODEL), lambda i: (i, 0)),
                  pl.BlockSpec((ROWS_TAIL, D_SB), lambda i: (i, 0)),
                  const((1, D_MODEL)),
                  const((D_MODEL, D_REST)),
                  const((1, D_SGU)), const((1, D_SGU)),
                  const((PAIRS, 2 * SGU_CHUNK, SGU_CHUNK)),
                  const((SGU_CHUNK, D_SGU)),
                  const((2 * D_SGU, LANES)),
                  const((D_SB, D_MODEL)), const((D_SGU, D_MODEL)), const((D_MODEL, D_MODEL)),
                  const((1, D_MODEL))],
        out_specs=pl.BlockSpec((ROWS_TAIL, D_MODEL), lambda i: (i, 0)),
        out_shape=jax.ShapeDtypeStruct((n, D_MODEL), F32),
        scratch_shapes=[pltpu.VMEM((ROWS_TAIL, D_SGU), F32)],
        compiler_params=pltpu.CompilerParams(
            dimension_semantics=("arbitrary",), vmem_limit_bytes=VMEM_LIMIT),
        name="tail",
    )(x2, o2, norm_g, w_rest, ln_g, ln_b, ws, bs, gm2, w_up_a, w_up_b, w_out, final_g)


def _layer(x, norm_g, w_in, ln_g, ln_b, w_s, b_s, w_up_a, w_up_b, w_out, final_g):
    b, s, _ = x.shape
    x2 = x.reshape(b * s, D_MODEL)
    q, k, v = _qkv_proj(x2, norm_g.reshape(1, D_MODEL), w_in)

    j = jnp.arange(BLK)
    tri = (j[:, None] >= j[None, :]).astype(BF16)
    tri2 = jnp.concatenate([tri, tri], axis=0)
    pair_major = lambda a: a.reshape(PAIRS, b, s, LANES)
    o, w_rest16, w_up_a16, w_up_b16, w_out16 = _attention(
        pair_major(q), pair_major(k), pair_major(v), tri2, w_in, w_up_a, w_up_b, w_out)

    gmean = (jnp.arange(D_SGU)[:, None] // GROUP_DIM == jnp.arange(LANES)[None, :] % N_GROUPS)
    gmean = (gmean.astype(F32) / GROUP_DIM).astype(BF16)
    gm2 = jnp.concatenate([gmean, gmean], axis=0)
    bias = jnp.repeat(b_s.T, GROUP_DIM, axis=1)
    return _tail(x2, o.reshape(b * s, D_SB), norm_g.reshape(1, D_MODEL), w_rest16,
                 ln_g.reshape(1, D_SGU), ln_b.reshape(1, D_SGU),
                 w_s.reshape(PAIRS, 2 * SGU_CHUNK, SGU_CHUNK), bias, gm2,
                 w_up_a16, w_up_b16, w_out16,
                 final_g.reshape(1, D_MODEL)).reshape(b, s, D_MODEL)


def kernel(x, norm_g, w_in, sgu_ln_g, sgu_ln_b, w_spatial, b_spatial, w_up_a, w_up_b, w_out,
           final_norm_g):
    assert norm_g.shape[0] == 1, "single-layer block"
    assert x.shape[1] % QROWS == 0 and (x.shape[0] * x.shape[1]) % ROWS_PROJ == 0
    return _layer(x, norm_g[0], w_in[0], sgu_ln_g[0], sgu_ln_b[0], w_spatial[0], b_spatial[0],
                  w_up_a[0], w_up_b[0], w_out[0], final_norm_g)
```

```python
import jax
import jax.numpy as jnp
from jax import lax
from jax.experimental import pallas as pl
from jax.experimental.pallas import tpu as pltpu

D_MODEL = 1024
N_HEADS = 8
HEAD_DIM = 64
D_SB = N_HEADS * HEAD_DIM
N_GROUPS = 8
GROUP_DIM = 64
D_SGU = N_GROUPS * GROUP_DIM
SGU_CHUNK = 128
CHUNK = 64
EPS = 1e-6
D_QKV = 3 * D_SB
D_REST = D_SB + 3 * D_SGU + 2 * D_MODEL

LANES = 128
BF16_SUBLANES = 16
PAIRS = D_SB // LANES

ROWS_PROJ = 1024
ROWS_TAIL = 1024
SUB_TAIL = 512
TAIL_SKEW = 1
BLK = 256
QROWS = 1024
QT = 128
TILES = QROWS // QT
CHAINS = TILES * PAIRS
GROUP = 8

VMEM_LIMIT = 56 * 1024 * 1024

F32 = jnp.float32
BF16 = jnp.bfloat16


def _rms_h(x, g):
    ms = jnp.mean(x * x, axis=-1, keepdims=True)
    return (x * lax.rsqrt(ms + EPS) * g).astype(BF16)


def _split_bf16(a):
    hi = a.astype(BF16)
    lo = (a - hi.astype(F32)).astype(BF16)
    return jnp.concatenate([hi, lo], axis=-1)


def _qkv_kernel(x_ref, g_ref, w_ref, q_ref, k_ref, v_ref, w16_ref):
    @pl.when(pl.program_id(0) == 0)
    def _():
        w16_ref[...] = w_ref[...].astype(BF16)

    scale = HEAD_DIM ** -0.5
    for r0 in range(0, ROWS_PROJ, ROWS_PROJ // 2):
        rows = slice(r0, r0 + ROWS_PROJ // 2)
        h = _rms_h(x_ref[rows, :], g_ref[...])
        for j, (ref, mul) in enumerate(((q_ref, scale), (k_ref, None), (v_ref, None))):
            y = jnp.dot(h, w16_ref[:, j * D_SB:(j + 1) * D_SB], preferred_element_type=F32)
            if mul is not None:
                y = y * mul
            y = y.astype(BF16)
            for p in range(PAIRS):
                ref[p, rows, :] = y[:, p * LANES:(p + 1) * LANES]


def _qkv_proj(x2, norm_g, w_in):
    n = x2.shape[0]
    out = jax.ShapeDtypeStruct((PAIRS, n, LANES), BF16)
    row_spec = pl.BlockSpec((ROWS_PROJ, D_MODEL), lambda i: (i, 0))
    out_spec = pl.BlockSpec((PAIRS, ROWS_PROJ, LANES), lambda i: (0, i, 0))
    return pl.pallas_call(
        _qkv_kernel,
        grid=(n // ROWS_PROJ,),
        in_specs=[row_spec,
                  pl.BlockSpec((1, D_MODEL), lambda i: (0, 0)),
                  pl.BlockSpec((D_MODEL, D_QKV), lambda i: (0, 0), pipeline_mode=pl.Buffered(1))],
        out_specs=[out_spec, out_spec, out_spec],
        out_shape=[out, out, out],
        scratch_shapes=[pltpu.VMEM((D_MODEL, D_QKV), BF16)],
        compiler_params=pltpu.CompilerParams(
            dimension_semantics=("arbitrary",), vmem_limit_bytes=VMEM_LIMIT),
        name="qkv_proj",
    )(x2, norm_g, w_in)


def _weight_slab_specs(steps, step_of):
    cols = D_REST // steps
    rows_up, rows_out = D_SB // steps, D_MODEL // steps
    assert cols % LANES == 0 and D_QKV % cols == 0 and rows_up % BF16_SUBLANES == 0, "slab tiling"
    rest_in = pl.BlockSpec((D_MODEL, cols), lambda *g: (0, D_QKV // cols + step_of(*g)))
    rest_out = pl.BlockSpec((D_MODEL, cols), lambda *g: (0, step_of(*g)))
    up = pl.BlockSpec((rows_up, D_MODEL), lambda *g: (step_of(*g), 0))
    out = pl.BlockSpec((rows_out, D_MODEL), lambda *g: (step_of(*g), 0))
    shapes = [jax.ShapeDtypeStruct((D_MODEL, D_REST), BF16),
              jax.ShapeDtypeStruct((D_SB, D_MODEL), BF16),
              jax.ShapeDtypeStruct((D_SGU, D_MODEL), BF16),
              jax.ShapeDtypeStruct((D_MODEL, D_MODEL), BF16)]
    return [rest_in, up, up, out], [rest_out, up, up, out], shapes


STICK_GONE = 88.0
MASKED = -1e30
SOFTPLUS_CLAMP = 40.0


def _softplus(z):
    return jnp.maximum(z, jnp.log(1.0 + jnp.exp(jnp.minimum(z, SOFTPLUS_CLAMP))))


def _attn_kernel(q_ref, k_ref, v_ref, tri_ref, wrest_ref, wua_ref, wub_ref, wo_ref,
                 o_ref, wrest16_ref, wua16_ref, wub16_ref, wo16_ref, acc_ref, carry_ref, more_ref):
    for src, dst in ((wrest_ref, wrest16_ref), (wua_ref, wua16_ref), (wub_ref, wub16_ref),
                     (wo_ref, wo16_ref)):
        dst[...] = src[...].astype(BF16)

    t0 = pl.program_id(1) * QROWS
    lane = lax.broadcasted_iota(jnp.int32, (QT, LANES), 1)
    first_head = lane < HEAD_DIM
    row = lax.broadcasted_iota(jnp.int32, (2 * QT, BLK), 0) & (QT - 1)
    col = lax.broadcasted_iota(jnp.int32, (2 * QT, BLK), 1)

    def q_chain(p, r0):
        qt = q_ref[p, 0, pl.ds(r0, QT), :]
        zero = jnp.zeros_like(qt)
        return jnp.concatenate([jnp.where(first_head, qt, zero), jnp.where(first_head, zero, qt)],
                               axis=0)

    def scores(p, start, q2, visible, first_keys_visible=False):
        kb = k_ref[p, 0, pl.ds(start, BLK), :]
        z = lax.dot_general(q2, kb, (((1,), (1,)), ((), ())), preferred_element_type=F32)
        if first_keys_visible:
            return jnp.concatenate(
                [z[:, :QT], jnp.where(visible[:, QT:], z[:, QT:], MASKED)], axis=1)
        return jnp.where(visible, z, MASKED)

    def stick(z):
        csum = jnp.dot(_split_bf16(_softplus(z)), tri_ref[...], preferred_element_type=F32)
        return z - csum, csum[:, 0:1]

    def values(p, start, log_w, total, carry):
        if carry is not None:
            log_w = log_w - jnp.concatenate([carry] * (BLK // LANES), axis=1)
        vb = v_ref[p, 0, pl.ds(start, BLK), :]
        pv = jnp.dot(jnp.exp(log_w).astype(BF16), vb, preferred_element_type=F32)
        return pv, jnp.broadcast_to(total, (2 * QT, LANES))

    def window_start(tile):
        return jnp.maximum(t0 + (tile + 1) * QT - BLK, 0)

    chains = [(tile, p) for tile in range(TILES) for p in range(PAIRS)]
    starts = [pl.multiple_of(window_start(tile), QT) for tile in range(TILES)]
    zs, sticks, least = {}, {}, {}

    def stage_scores(c):
        tile, p = chains[c]
        visible = col - row < t0 + tile * QT - starts[tile]
        zs[c] = scores(p, starts[tile], q_chain(p, tile * QT), visible, tile >= 1)

    def stage_stick(c):
        sticks[c] = stick(zs[c])

    def stage_values(c):
        tile, p = chains[c]
        pv, tot = values(p, starts[tile], *sticks[c], None)
        acc_ref[c] = pv
        carry_ref[c] = tot
        least[c] = jnp.min(tot, axis=0, keepdims=True)

    stages = (stage_scores, stage_stick, stage_values)
    n_groups = CHAINS // GROUP
    for slot in range(n_groups + len(stages) - 1):
        for s, stage in enumerate(stages):
            g = slot - s
            if 0 <= g < n_groups:
                for c in range(g * GROUP, (g + 1) * GROUP):
                    stage(c)
    n_more = jnp.int32(0)
    for c, (tile, p) in enumerate(chains):
        more_ref[n_more] = jnp.int32(c)
        listed = jnp.logical_and(starts[tile] > 0, least[c][0, 0] < STICK_GONE)
        n_more = n_more + listed.astype(jnp.int32)

    def older_keys(j, _):
        c = more_ref[j]
        tile = c // PAIRS
        p = c % PAIRS
        q2 = q_chain(p, pl.multiple_of(tile * QT, QT))

        def body(state):
            end, _ = state
            start = pl.multiple_of(jnp.maximum(end - BLK, 0), QT)
            pv, tot = values(p, start, *stick(scores(p, start, q2, col < end - start)),
                             carry_ref[c])
            acc_ref[c] += pv
            carry = carry_ref[c] + tot
            carry_ref[c] = carry
            return start, jnp.logical_and(start > 0, jnp.min(carry) < STICK_GONE)

        lax.while_loop(lambda state: state[1], body, (window_start(tile), True))
        return 0

    lax.fori_loop(0, n_more, older_keys, 0)

    for tile in range(TILES):
        for p in range(PAIRS):
            a = acc_ref[tile * PAIRS + p]
            o_ref[0, tile * QT:(tile + 1) * QT, p * LANES:(p + 1) * LANES] = jnp.where(
                first_head, a[0:QT], a[QT:2 * QT])


def _attention(q, k, v, tri, w_in, w_up_a, w_up_b, w_out):
    _, b, s, _ = q.shape
    steps_per_batch = s // QROWS
    slab_in, slab_out, slab_shapes = _weight_slab_specs(
        b * steps_per_batch, lambda bi, i: bi * steps_per_batch + i)
    seq_spec = pl.BlockSpec((PAIRS, 1, s, LANES), lambda bi, i: (0, bi, 0, 0))
    return pl.pallas_call(
        _attn_kernel,
        grid=(b, s // QROWS),
        in_specs=[pl.BlockSpec((PAIRS, 1, QROWS, LANES), lambda bi, i: (0, bi, i, 0)),
                  seq_spec, seq_spec,
                  pl.BlockSpec((2 * BLK, BLK), lambda bi, i: (0, 0))] + slab_in,
        out_specs=[pl.BlockSpec((1, QROWS, D_SB), lambda bi, i: (bi, i, 0))] + slab_out,
        out_shape=[jax.ShapeDtypeStruct((b, s, D_SB), F32)] + slab_shapes,
        scratch_shapes=[pltpu.VMEM((CHAINS, 2 * QT, LANES), F32),
                        pltpu.VMEM((CHAINS, 2 * QT, LANES), F32),
                        pltpu.SMEM((CHAINS,), jnp.int32)],
        compiler_params=pltpu.CompilerParams(
            dimension_semantics=("arbitrary", "arbitrary"),
            vmem_limit_bytes=VMEM_LIMIT),
        name="stickbreak",
    )(q, k, v, tri, w_in, w_up_a, w_up_b, w_out)


def _tail_kernel(x_ref, o_ref, ng_ref, w_ref, lng_ref, lnb_ref, ws_ref, bs_ref, gm_ref,
                 wua_ref, wub_ref, wo_ref, fg_ref, out_ref, mixed_ref):
    tiles = [slice(r0, r0 + SUB_TAIL) for r0 in range(0, ROWS_TAIL, SUB_TAIL)]
    c_za, c_ub, c_vb, c_zb, c_ga, c_gb = 0, 512, 1024, 1536, 2048, 3072
    half = lax.broadcasted_iota(jnp.int32, (SUB_TAIL, LANES), 1) // GROUP_DIM
    pos_t = lax.broadcasted_iota(jnp.int32, (2 * SGU_CHUNK, SGU_CHUNK), 0) % SGU_CHUNK
    pos_s = lax.broadcasted_iota(jnp.int32, (2 * SGU_CHUNK, SGU_CHUNK), 1)
    causal = (pos_s // CHUNK) <= (pos_t // CHUNK)
    lane = lax.broadcasted_iota(jnp.int32, (SGU_CHUNK, LANES), 1)
    first_group = lane < GROUP_DIM

    def spread(stats):
        return jnp.concatenate(
            [jnp.take_along_axis(stats, 2 * p + half, axis=1) for p in range(PAIRS)], axis=1)

    state = [dict() for _ in tiles]

    def proj(s, lo, hi):
        return jnp.dot(s["h"], w_ref[:, lo:hi], preferred_element_type=F32)

    def stage0(s, t):
        s["x"] = x_ref[t, :]
        s["h"] = _rms_h(s["x"], ng_ref[...])
        s["vg"] = jax.nn.gelu(proj(s, c_vb, c_zb))

    def stage1(s, t):
        mu = spread(jnp.dot(_split_bf16(s["vg"]), gm_ref[...], preferred_element_type=F32))
        s["d"] = s["vg"] - mu
        s["ug"] = jax.nn.gelu(proj(s, c_ub, c_vb))

    def stage2(s, t):
        d = s["d"]
        var = spread(jnp.dot((d * d).astype(BF16), gm_ref[0:D_SGU, :], preferred_element_type=F32))
        s["vn"] = (d * lax.rsqrt(var + EPS) * lng_ref[...] + lnb_ref[...]).astype(BF16)
        s["sz_b"] = jax.nn.silu(proj(s, c_zb, c_ga))

    def stage3(s, t):
        s["sz_a"] = jax.nn.silu(proj(s, c_za, c_ub))
        vn = s["vn"]
        for p in range(PAIRS):
            wsp = jnp.where(causal, ws_ref[p], 0.0).astype(BF16)
            lanes = slice(p * LANES, (p + 1) * LANES)
            for c in range(0, SUB_TAIL, 2 * SGU_CHUNK):
                rows = [slice(c + j * SGU_CHUNK, c + (j + 1) * SGU_CHUNK) for j in range(2)]
                r = jnp.dot(wsp, jnp.concatenate([vn[rows[0], lanes], vn[rows[1], lanes]], axis=1),
                            preferred_element_type=F32)
                for j in range(2):
                    rj = r[:, j * LANES:(j + 1) * LANES]
                    mixed_ref[t.start + rows[j].start:t.start + rows[j].stop, lanes] = jnp.where(
                        first_group, rj[0:SGU_CHUNK], rj[SGU_CHUNK:2 * SGU_CHUNK])

    def stage4(s, t):
        p_a = jnp.dot((o_ref[t, :] * s["sz_a"]).astype(BF16), wua_ref[...],
                      preferred_element_type=F32)
        s["merged"] = jax.nn.sigmoid(proj(s, c_ga, c_gb)) * p_a

    def stage5(s, t):
        bias = jnp.concatenate([bs_ref[...]] * (SUB_TAIL // SGU_CHUNK), axis=0)
        p_b = jnp.dot((s["ug"] * (mixed_ref[t, :] + bias) * s["sz_b"]).astype(BF16), wub_ref[...],
                      preferred_element_type=F32)
        s["merged"] = s["merged"] + jax.nn.sigmoid(proj(s, c_gb, D_REST)) * p_b

    def stage6(s, t):
        y = s["x"] + jnp.dot(s["merged"].astype(BF16), wo_ref[...], preferred_element_type=F32)
        ms = jnp.mean(y * y, axis=-1, keepdims=True)
        out_ref[t, :] = y * lax.rsqrt(ms + EPS) * fg_ref[...]

    stages = (stage0, stage1, stage2, stage3, stage4, stage5, stage6)
    for slot in range(len(stages) + (len(tiles) - 1) * TAIL_SKEW):
        for i, t in enumerate(tiles):
            k = slot - i * TAIL_SKEW
            if 0 <= k < len(stages):
                stages[k](state[i], t)


def _tail(x2, o2, norm_g, w_rest, ln_g, ln_b, ws, bs, gm2, w_up_a, w_up_b, w_out, final_g):
    n = x2.shape[0]
    const = lambda shape: pl.BlockSpec(shape, lambda i: (0,) * len(shape),
                                       pipeline_mode=pl.Buffered(1))
    return pl.pallas_call(
        _tail_kernel,
        grid=(n // ROWS_TAIL,),
        in_specs=[pl.BlockSpec((ROWS_TAIL, D_MODEL), lambda i: (i, 0)),
                  pl.BlockSpec((ROWS_TAIL, D_SB), lambda i: (i, 0)),
                  const((1, D_MODEL)),
                  const((D_MODEL, D_REST)),
                  const((1, D_SGU)), const((1, D_SGU)),
                  const((PAIRS, 2 * SGU_CHUNK, SGU_CHUNK)),
                  const((SGU_CHUNK, D_SGU)),
                  const((2 * D_SGU, LANES)),
                  const((D_SB, D_MODEL)), const((D_SGU, D_MODEL)), const((D_MODEL, D_MODEL)),
                  const((1, D_MODEL))],
        out_specs=pl.BlockSpec((ROWS_TAIL, D_MODEL), lambda i: (i, 0)),
        out_shape=jax.ShapeDtypeStruct((n, D_MODEL), F32),
        scratch_shapes=[pltpu.VMEM((ROWS_TAIL, D_SGU), F32)],
        compiler_params=pltpu.CompilerParams(
            dimension_semantics=("arbitrary",), vmem_limit_bytes=VMEM_LIMIT),
        name="tail",
    )(x2, o2, norm_g, w_rest, ln_g, ln_b, ws, bs, gm2, w_up_a, w_up_b, w_out, final_g)


def _layer(x, norm_g, w_in, ln_g, ln_b, w_s, b_s, w_up_a, w_up_b, w_out, final_g):
    b, s, _ = x.shape
    x2 = x.reshape(b * s, D_MODEL)
    q, k, v = _qkv_proj(x2, norm_g.reshape(1, D_MODEL), w_in)

    j = jnp.arange(BLK)
    tri = (j[:, None] >= j[None, :]).astype(BF16)
    tri2 = jnp.concatenate([tri, tri], axis=0)
    pair_major = lambda a: a.reshape(PAIRS, b, s, LANES)
    o, w_rest16, w_up_a16, w_up_b16, w_out16 = _attention(
        pair_major(q), pair_major(k), pair_major(v), tri2, w_in, w_up_a, w_up_b, w_out)

    gmean = (jnp.arange(D_SGU)[:, None] // GROUP_DIM == jnp.arange(LANES)[None, :] % N_GROUPS)
    gmean = (gmean.astype(F32) / GROUP_DIM).astype(BF16)
    gm2 = jnp.concatenate([gmean, gmean], axis=0)
    bias = jnp.repeat(b_s.T, GROUP_DIM, axis=1)
    return _tail(x2, o.reshape(b * s, D_SB), norm_g.reshape(1, D_MODEL), w_rest16,
                 ln_g.reshape(1, D_SGU), ln_b.reshape(1, D_SGU),
                 w_s.reshape(PAIRS, 2 * SGU_CHUNK, SGU_CHUNK), bias, gm2,
                 w_up_a16, w_up_b16, w_out16,
                 final_g.reshape(1, D_MODEL)).reshape(b, s, D_MODEL)


def kernel(x, norm_g, w_in, sgu_ln_g, sgu_ln_b, w_spatial, b_spatial, w_up_a, w_up_b, w_out,
           final_norm_g):
    assert norm_g.shape[0] == 1, "single-layer block"
    assert x.shape[1] % QROWS == 0 and (x.shape[0] * x.shape[1]) % ROWS_PROJ == 0
    return _layer(x, norm_g[0], w_in[0], sgu_ln_g[0], sgu_ln_b[0], w_spatial[0], b_spatial[0],
                  w_up_a[0], w_up_b[0], w_out[0], final_norm_g)
```

```python
import jax
import jax.numpy as jnp
from jax import lax
from jax.experimental import pallas as pl
from jax.experimental.pallas import tpu as pltpu

D_MODEL = 1024
N_HEADS = 8
HEAD_DIM = 64
D_SB = N_HEADS * HEAD_DIM
N_GROUPS = 8
GROUP_DIM = 64
D_SGU = N_GROUPS * GROUP_DIM
SGU_CHUNK = 128
CHUNK = 64
EPS = 1e-6
D_QKV = 3 * D_SB
D_REST = D_SB + 3 * D_SGU + 2 * D_MODEL

LANES = 128
BF16_SUBLANES = 16
PAIRS = D_SB // LANES

ROWS_PROJ = 1024
ROWS_TAIL = 1024
SUB_TAIL = 512
TAIL_SKEW = 2
BLK = 256
QROWS = 1024
QT = 128
TILES = QROWS // QT
CHAINS = TILES * PAIRS
GROUP = 8

VMEM_LIMIT = 56 * 1024 * 1024

F32 = jnp.float32
BF16 = jnp.bfloat16


def _rms_h(x, g):
    ms = jnp.mean(x * x, axis=-1, keepdims=True)
    return (x * lax.rsqrt(ms + EPS) * g).astype(BF16)


def _split_bf16(a):
    hi = a.astype(BF16)
    lo = (a - hi.astype(F32)).astype(BF16)
    return jnp.concatenate([hi, lo], axis=-1)


def _qkv_kernel(x_ref, g_ref, w_ref, q_ref, k_ref, v_ref, w16_ref):
    @pl.when(pl.program_id(0) == 0)
    def _():
        w16_ref[...] = w_ref[...].astype(BF16)

    scale = HEAD_DIM ** -0.5
    for r0 in range(0, ROWS_PROJ, ROWS_PROJ // 2):
        rows = slice(r0, r0 + ROWS_PROJ // 2)
        h = _rms_h(x_ref[rows, :], g_ref[...])
        for j, (ref, mul) in enumerate(((q_ref, scale), (k_ref, None), (v_ref, None))):
            y = jnp.dot(h, w16_ref[:, j * D_SB:(j + 1) * D_SB], preferred_element_type=F32)
            if mul is not None:
                y = y * mul
            y = y.astype(BF16)
            for p in range(PAIRS):
                ref[p, rows, :] = y[:, p * LANES:(p + 1) * LANES]


def _qkv_proj(x2, norm_g, w_in):
    n = x2.shape[0]
    out = jax.ShapeDtypeStruct((PAIRS, n, LANES), BF16)
    row_spec = pl.BlockSpec((ROWS_PROJ, D_MODEL), lambda i: (i, 0))
    out_spec = pl.BlockSpec((PAIRS, ROWS_PROJ, LANES), lambda i: (0, i, 0))
    return pl.pallas_call(
        _qkv_kernel,
        grid=(n // ROWS_PROJ,),
        in_specs=[row_spec,
                  pl.BlockSpec((1, D_MODEL), lambda i: (0, 0)),
                  pl.BlockSpec((D_MODEL, D_QKV), lambda i: (0, 0), pipeline_mode=pl.Buffered(1))],
        out_specs=[out_spec, out_spec, out_spec],
        out_shape=[out, out, out],
        scratch_shapes=[pltpu.VMEM((D_MODEL, D_QKV), BF16)],
        compiler_params=pltpu.CompilerParams(
            dimension_semantics=("arbitrary",), vmem_limit_bytes=VMEM_LIMIT),
        name="qkv_proj",
    )(x2, norm_g, w_in)


def _weight_slab_specs(steps, step_of):
    cols = D_REST // steps
    rows_up, rows_out = D_SB // steps, D_MODEL // steps
    assert cols % LANES == 0 and D_QKV % cols == 0 and rows_up % BF16_SUBLANES == 0, "slab tiling"
    rest_in = pl.BlockSpec((D_MODEL, cols), lambda *g: (0, D_QKV // cols + step_of(*g)))
    rest_out = pl.BlockSpec((D_MODEL, cols), lambda *g: (0, step_of(*g)))
    up = pl.BlockSpec((rows_up, D_MODEL), lambda *g: (step_of(*g), 0))
    out = pl.BlockSpec((rows_out, D_MODEL), lambda *g: (step_of(*g), 0))
    shapes = [jax.ShapeDtypeStruct((D_MODEL, D_REST), BF16),
              jax.ShapeDtypeStruct((D_SB, D_MODEL), BF16),
              jax.ShapeDtypeStruct((D_SGU, D_MODEL), BF16),
              jax.ShapeDtypeStruct((D_MODEL, D_MODEL), BF16)]
    return [rest_in, up, up, out], [rest_out, up, up, out], shapes


STICK_GONE = 88.0
MASKED = -1e30
SOFTPLUS_CLAMP = 40.0


def _softplus(z):
    return jnp.maximum(z, jnp.log(1.0 + jnp.exp(jnp.minimum(z, SOFTPLUS_CLAMP))))


def _attn_kernel(q_ref, k_ref, v_ref, tri_ref, wrest_ref, wua_ref, wub_ref, wo_ref,
                 o_ref, wrest16_ref, wua16_ref, wub16_ref, wo16_ref, acc_ref, carry_ref, more_ref):
    for src, dst in ((wrest_ref, wrest16_ref), (wua_ref, wua16_ref), (wub_ref, wub16_ref),
                     (wo_ref, wo16_ref)):
        dst[...] = src[...].astype(BF16)

    t0 = pl.program_id(1) * QROWS
    lane = lax.broadcasted_iota(jnp.int32, (QT, LANES), 1)
    first_head = lane < HEAD_DIM
    row = lax.broadcasted_iota(jnp.int32, (2 * QT, BLK), 0) & (QT - 1)
    col = lax.broadcasted_iota(jnp.int32, (2 * QT, BLK), 1)

    def q_chain(p, r0):
        qt = q_ref[p, 0, pl.ds(r0, QT), :]
        zero = jnp.zeros_like(qt)
        return jnp.concatenate([jnp.where(first_head, qt, zero), jnp.where(first_head, zero, qt)],
                               axis=0)

    def scores(p, start, q2, visible, first_keys_visible=False):
        kb = k_ref[p, 0, pl.ds(start, BLK), :]
        z = lax.dot_general(q2, kb, (((1,), (1,)), ((), ())), preferred_element_type=F32)
        if first_keys_visible:
            return jnp.concatenate(
                [z[:, :QT], jnp.where(visible[:, QT:], z[:, QT:], MASKED)], axis=1)
        return jnp.where(visible, z, MASKED)

    def stick(z):
        csum = jnp.dot(_split_bf16(_softplus(z)), tri_ref[...], preferred_element_type=F32)
        return z - csum, csum[:, 0:1]

    def values(p, start, log_w, total, carry):
        if carry is not None:
            log_w = log_w - jnp.concatenate([carry] * (BLK // LANES), axis=1)
        vb = v_ref[p, 0, pl.ds(start, BLK), :]
        pv = jnp.dot(jnp.exp(log_w).astype(BF16), vb, preferred_element_type=F32)
        return pv, jnp.broadcast_to(total, (2 * QT, LANES))

    def window_start(tile):
        return jnp.maximum(t0 + (tile + 1) * QT - BLK, 0)

    chains = [(tile, p) for tile in range(TILES) for p in range(PAIRS)]
    starts = [pl.multiple_of(window_start(tile), QT) for tile in range(TILES)]
    zs, sticks, least = {}, {}, {}

    def stage_scores(c):
        tile, p = chains[c]
        visible = col - row < t0 + tile * QT - starts[tile]
        zs[c] = scores(p, starts[tile], q_chain(p, tile * QT), visible, tile >= 1)

    def stage_stick(c):
        sticks[c] = stick(zs[c])

    def stage_values(c):
        tile, p = chains[c]
        pv, tot = values(p, starts[tile], *sticks[c], None)
        acc_ref[c] = pv
        carry_ref[c] = tot
        least[c] = jnp.min(tot, axis=0, keepdims=True)

    stages = (stage_scores, stage_stick, stage_values)
    n_groups = CHAINS // GROUP
    for slot in range(n_groups + len(stages) - 1):
        for s, stage in enumerate(stages):
            g = slot - s
            if 0 <= g < n_groups:
                for c in range(g * GROUP, (g + 1) * GROUP):
                    stage(c)
    n_more = jnp.int32(0)
    for c, (tile, p) in enumerate(chains):
        more_ref[n_more] = jnp.int32(c)
        listed = jnp.logical_and(starts[tile] > 0, least[c][0, 0] < STICK_GONE)
        n_more = n_more + listed.astype(jnp.int32)

    def older_keys(j, _):
        c = more_ref[j]
        tile = c // PAIRS
        p = c % PAIRS
        q2 = q_chain(p, pl.multiple_of(tile * QT, QT))

        def body(state):
            end, _ = state
            start = pl.multiple_of(jnp.maximum(end - BLK, 0), QT)
            pv, tot = values(p, start, *stick(scores(p, start, q2, col < end - start)),
                             carry_ref[c])
            acc_ref[c] += pv
            carry = carry_ref[c] + tot
            carry_ref[c] = carry
            return start, jnp.logical_and(start > 0, jnp.min(carry) < STICK_GONE)

        lax.while_loop(lambda state: state[1], body, (window_start(tile), True))
        return 0

    lax.fori_loop(0, n_more, older_keys, 0)

    for tile in range(TILES):
        for p in range(PAIRS):
            a = acc_ref[tile * PAIRS + p]
            o_ref[0, tile * QT:(tile + 1) * QT, p * LANES:(p + 1) * LANES] = jnp.where(
                first_head, a[0:QT], a[QT:2 * QT])


def _attention(q, k, v, tri, w_in, w_up_a, w_up_b, w_out):
    _, b, s, _ = q.shape
    steps_per_batch = s // QROWS
    slab_in, slab_out, slab_shapes = _weight_slab_specs(
        b * steps_per_batch, lambda bi, i: bi * steps_per_batch + i)
    seq_spec = pl.BlockSpec((PAIRS, 1, s, LANES), lambda bi, i: (0, bi, 0, 0))
    return pl.pallas_call(
        _attn_kernel,
        grid=(b, s // QROWS),
        in_specs=[pl.BlockSpec((PAIRS, 1, QROWS, LANES), lambda bi, i: (0, bi, i, 0)),
                  seq_spec, seq_spec,
                  pl.BlockSpec((2 * BLK, BLK), lambda bi, i: (0, 0))] + slab_in,
        out_specs=[pl.BlockSpec((1, QROWS, D_SB), lambda bi, i: (bi, i, 0))] + slab_out,
        out_shape=[jax.ShapeDtypeStruct((b, s, D_SB), F32)] + slab_shapes,
        scratch_shapes=[pltpu.VMEM((CHAINS, 2 * QT, LANES), F32),
                        pltpu.VMEM((CHAINS, 2 * QT, LANES), F32),
                        pltpu.SMEM((CHAINS,), jnp.int32)],
        compiler_params=pltpu.CompilerParams(
            dimension_semantics=("arbitrary", "arbitrary"),
            vmem_limit_bytes=VMEM_LIMIT),
        name="stickbreak",
    )(q, k, v, tri, w_in, w_up_a, w_up_b, w_out)


def _tail_kernel(x_ref, o_ref, ng_ref, w_ref, lng_ref, lnb_ref, ws_ref, bs_ref, gm_ref,
                 wua_ref, wub_ref, wo_ref, fg_ref, out_ref, mixed_ref):
    tiles = [slice(r0, r0 + SUB_TAIL) for r0 in range(0, ROWS_TAIL, SUB_TAIL)]
    c_za, c_ub, c_vb, c_zb, c_ga, c_gb = 0, 512, 1024, 1536, 2048, 3072
    half = lax.broadcasted_iota(jnp.int32, (SUB_TAIL, LANES), 1) // GROUP_DIM
    pos_t = lax.broadcasted_iota(jnp.int32, (2 * SGU_CHUNK, SGU_CHUNK), 0) % SGU_CHUNK
    pos_s = lax.broadcasted_iota(jnp.int32, (2 * SGU_CHUNK, SGU_CHUNK), 1)
    causal = (pos_s // CHUNK) <= (pos_t // CHUNK)
    lane = lax.broadcasted_iota(jnp.int32, (SGU_CHUNK, LANES), 1)
    first_group = lane < GROUP_DIM

    def spread(stats):
        return jnp.concatenate(
            [jnp.take_along_axis(stats, 2 * p + half, axis=1) for p in range(PAIRS)], axis=1)

    state = [dict() for _ in tiles]

    def proj(s, lo, hi):
        return jnp.dot(s["h"], w_ref[:, lo:hi], preferred_element_type=F32)

    def norm_vb(s, t):
        s["x"] = x_ref[t, :]
        s["h"] = _rms_h(s["x"], ng_ref[...])
        s["vg"] = jax.nn.gelu(proj(s, c_vb, c_zb))

    def mean(s, t):
        mu = spread(jnp.dot(_split_bf16(s["vg"]), gm_ref[...], preferred_element_type=F32))
        s["d"] = s["vg"] - mu

    def ub(s, t):
        s["ug"] = jax.nn.gelu(proj(s, c_ub, c_vb))

    def variance(s, t):
        d = s["d"]
        var = spread(jnp.dot((d * d).astype(BF16), gm_ref[0:D_SGU, :], preferred_element_type=F32))
        s["vn"] = (d * lax.rsqrt(var + EPS) * lng_ref[...] + lnb_ref[...]).astype(BF16)

    def zb(s, t):
        s["sz_b"] = jax.nn.silu(proj(s, c_zb, c_ga))

    def za(s, t):
        s["sz_a"] = jax.nn.silu(proj(s, c_za, c_ub))

    def mix(s, t):
        vn = s["vn"]
        for p in range(PAIRS):
            wsp = jnp.where(causal, ws_ref[p], 0.0).astype(BF16)
            lanes = slice(p * LANES, (p + 1) * LANES)
            for c in range(0, SUB_TAIL, 2 * SGU_CHUNK):
                rows = [slice(c + j * SGU_CHUNK, c + (j + 1) * SGU_CHUNK) for j in range(2)]
                r = jnp.dot(wsp, jnp.concatenate([vn[rows[0], lanes], vn[rows[1], lanes]], axis=1),
                            preferred_element_type=F32)
                for j in range(2):
                    rj = r[:, j * LANES:(j + 1) * LANES]
                    mixed_ref[t.start + rows[j].start:t.start + rows[j].stop, lanes] = jnp.where(
                        first_group, rj[0:SGU_CHUNK], rj[SGU_CHUNK:2 * SGU_CHUNK])

    def up_a(s, t):
        s["p_a"] = jnp.dot((o_ref[t, :] * s["sz_a"]).astype(BF16), wua_ref[...],
                           preferred_element_type=F32)

    def gate_a(s, t):
        s["merged"] = jax.nn.sigmoid(proj(s, c_ga, c_gb)) * s["p_a"]

    def up_b(s, t):
        bias = jnp.concatenate([bs_ref[...]] * (SUB_TAIL // SGU_CHUNK), axis=0)
        s["p_b"] = jnp.dot((s["ug"] * (mixed_ref[t, :] + bias) * s["sz_b"]).astype(BF16),
                           wub_ref[...], preferred_element_type=F32)

    def gate_b(s, t):
        s["merged"] = s["merged"] + jax.nn.sigmoid(proj(s, c_gb, D_REST)) * s["p_b"]

    def out(s, t):
        y = s["x"] + jnp.dot(s["merged"].astype(BF16), wo_ref[...], preferred_element_type=F32)
        ms = jnp.mean(y * y, axis=-1, keepdims=True)
        out_ref[t, :] = y * lax.rsqrt(ms + EPS) * fg_ref[...]

    stages = (norm_vb, mean, ub, variance, zb, za, mix, up_a, gate_a, up_b, gate_b, out)
    for slot in range(len(stages) + (len(tiles) - 1) * TAIL_SKEW):
        for i, t in enumerate(tiles):
            k = slot - i * TAIL_SKEW
            if 0 <= k < len(stages):
                stages[k](state[i], t)


def _tail(x2, o2, norm_g, w_rest, ln_g, ln_b, ws, bs, gm2, w_up_a, w_up_b, w_out, final_g):
    n = x2.shape[0]
    const = lambda shape: pl.BlockSpec(shape, lambda i: (0,) * len(shape),
                                       pipeline_mode=pl.Buffered(1))
    return pl.pallas_call(
        _tail_kernel,
        grid=(n // ROWS_TAIL,),
        in_specs=[pl.BlockSpec((ROWS_TAIL, D_MODEL), lambda i: (i, 0)),
                  pl.BlockSpec((ROWS_TAIL, D_SB), lambda i: (i, 0)),
                  const((1, D_MODEL)),
                  const((D_MODEL, D_REST)),
                  const((1, D_SGU)), const((1, D_SGU)),
                  const((PAIRS, 2 * SGU_CHUNK, SGU_CHUNK)),
                  const((SGU_CHUNK, D_SGU)),
                  const((2 * D_SGU, LANES)),
                  const((D_SB, D_MODEL)), const((D_SGU, D_MODEL)), const((D_MODEL, D_MODEL)),
                  const((1, D_MODEL))],
        out_specs=pl.BlockSpec((ROWS_TAIL, D_MODEL), lambda i: (i, 0)),
        out_shape=jax.ShapeDtypeStruct((n, D_MODEL), F32),
        scratch_shapes=[pltpu.VMEM((ROWS_TAIL, D_SGU), F32)],
        compiler_params=pltpu.CompilerParams(
            dimension_semantics=("arbitrary",), vmem_limit_bytes=VMEM_LIMIT),
        name="tail",
    )(x2, o2, norm_g, w_rest, ln_g, ln_b, ws, bs, gm2, w_up_a, w_up_b, w_out, final_g)


def _layer(x, norm_g, w_in, ln_g, ln_b, w_s, b_s, w_up_a, w_up_b, w_out, final_g):
    b, s, _ = x.shape
    x2 = x.reshape(b * s, D_MODEL)
    q, k, v = _qkv_proj(x2, norm_g.reshape(1, D_MODEL), w_in)

    j = jnp.arange(BLK)
    tri = (j[:, None] >= j[None, :]).astype(BF16)
    tri2 = jnp.concatenate([tri, tri], axis=0)
    pair_major = lambda a: a.reshape(PAIRS, b, s, LANES)
    o, w_rest16, w_up_a16, w_up_b16, w_out16 = _attention(
        pair_major(q), pair_major(k), pair_major(v), tri2, w_in, w_up_a, w_up_b, w_out)

    gmean = (jnp.arange(D_SGU)[:, None] // GROUP_DIM == jnp.arange(LANES)[None, :] % N_GROUPS)
    gmean = (gmean.astype(F32) / GROUP_DIM).astype(BF16)
    gm2 = jnp.concatenate([gmean, gmean], axis=0)
    bias = jnp.repeat(b_s.T, GROUP_DIM, axis=1)
    return _tail(x2, o.reshape(b * s, D_SB), norm_g.reshape(1, D_MODEL), w_rest16,
                 ln_g.reshape(1, D_SGU), ln_b.reshape(1, D_SGU),
                 w_s.reshape(PAIRS, 2 * SGU_CHUNK, SGU_CHUNK), bias, gm2,
                 w_up_a16, w_up_b16, w_out16,
                 final_g.reshape(1, D_MODEL)).reshape(b, s, D_MODEL)


def kernel(x, norm_g, w_in, sgu_ln_g, sgu_ln_b, w_spatial, b_spatial, w_up_a, w_up_b, w_out,
           final_norm_g):
    assert norm_g.shape[0] == 1, "single-layer block"
    assert x.shape[1] % QROWS == 0 and (x.shape[0] * x.shape[1]) % ROWS_PROJ == 0
    return _layer(x, norm_g[0], w_in[0], sgu_ln_g[0], sgu_ln_b[0], w_spatial[0], b_spatial[0],
                  w_up_a[0], w_up_b[0], w_out[0], final_norm_g)
```

```python
import jax
import jax.numpy as jnp
from jax import lax
from jax.experimental import pallas as pl
from jax.experimental.pallas import tpu as pltpu

D_MODEL = 1024
N_HEADS = 8
HEAD_DIM = 64
D_SB = N_HEADS * HEAD_DIM
N_GROUPS = 8
GROUP_DIM = 64
D_SGU = N_GROUPS * GROUP_DIM
SGU_CHUNK = 128
CHUNK = 64
EPS = 1e-6
D_QKV = 3 * D_SB
D_REST = D_SB + 3 * D_SGU + 2 * D_MODEL

LANES = 128
BF16_SUBLANES = 16
PAIRS = D_SB // LANES

ROWS_PROJ = 1024
ROWS_TAIL = 1024
SUB_TAIL = 512
TAIL_SKEW = 2
BLK = 256
QROWS = 1024
QT = 128
TILES = QROWS // QT
CHAINS = TILES * PAIRS
GROUP = 8

VMEM_LIMIT = 56 * 1024 * 1024

F32 = jnp.float32
BF16 = jnp.bfloat16


def _rms_h(x, g):
    ms = jnp.mean(x * x, axis=-1, keepdims=True)
    return (x * lax.rsqrt(ms + EPS) * g).astype(BF16)


def _split_bf16(a):
    hi = a.astype(BF16)
    lo = (a - hi.astype(F32)).astype(BF16)
    return jnp.concatenate([hi, lo], axis=-1)


def _qkv_kernel(x_ref, g_ref, w_ref, q_ref, k_ref, v_ref, w16_ref):
    @pl.when(pl.program_id(0) == 0)
    def _():
        w16_ref[...] = w_ref[...].astype(BF16)

    scale = HEAD_DIM ** -0.5
    for r0 in range(0, ROWS_PROJ, ROWS_PROJ // 2):
        rows = slice(r0, r0 + ROWS_PROJ // 2)
        h = _rms_h(x_ref[rows, :], g_ref[...])
        for j, (ref, mul) in enumerate(((q_ref, scale), (k_ref, None), (v_ref, None))):
            y = jnp.dot(h, w16_ref[:, j * D_SB:(j + 1) * D_SB], preferred_element_type=F32)
            if mul is not None:
                y = y * mul
            y = y.astype(BF16)
            for p in range(PAIRS):
                ref[p, rows, :] = y[:, p * LANES:(p + 1) * LANES]


def _qkv_proj(x2, norm_g, w_in):
    n = x2.shape[0]
    out = jax.ShapeDtypeStruct((PAIRS, n, LANES), BF16)
    row_spec = pl.BlockSpec((ROWS_PROJ, D_MODEL), lambda i: (i, 0))
    out_spec = pl.BlockSpec((PAIRS, ROWS_PROJ, LANES), lambda i: (0, i, 0))
    return pl.pallas_call(
        _qkv_kernel,
        grid=(n // ROWS_PROJ,),
        in_specs=[row_spec,
                  pl.BlockSpec((1, D_MODEL), lambda i: (0, 0)),
                  pl.BlockSpec((D_MODEL, D_QKV), lambda i: (0, 0), pipeline_mode=pl.Buffered(1))],
        out_specs=[out_spec, out_spec, out_spec],
        out_shape=[out, out, out],
        scratch_shapes=[pltpu.VMEM((D_MODEL, D_QKV), BF16)],
        compiler_params=pltpu.CompilerParams(
            dimension_semantics=("arbitrary",), vmem_limit_bytes=VMEM_LIMIT),
        name="qkv_proj",
    )(x2, norm_g, w_in)


def _weight_slab_specs(steps, step_of):
    cols = D_REST // steps
    rows_up, rows_out = D_SB // steps, D_MODEL // steps
    assert cols % LANES == 0 and D_QKV % cols == 0 and rows_up % BF16_SUBLANES == 0, "slab tiling"
    rest_in = pl.BlockSpec((D_MODEL, cols), lambda *g: (0, D_QKV // cols + step_of(*g)))
    rest_out = pl.BlockSpec((D_MODEL, cols), lambda *g: (0, step_of(*g)))
    up = pl.BlockSpec((rows_up, D_MODEL), lambda *g: (step_of(*g), 0))
    out = pl.BlockSpec((rows_out, D_MODEL), lambda *g: (step_of(*g), 0))
    shapes = [jax.ShapeDtypeStruct((D_MODEL, D_REST), BF16),
              jax.ShapeDtypeStruct((D_SB, D_MODEL), BF16),
              jax.ShapeDtypeStruct((D_SGU, D_MODEL), BF16),
              jax.ShapeDtypeStruct((D_MODEL, D_MODEL), BF16)]
    return [rest_in, up, up, out], [rest_out, up, up, out], shapes


STICK_GONE = 88.0
MASKED = -1e30
SOFTPLUS_CLAMP = 40.0


def _softplus(z):
    return jnp.maximum(z, jnp.log(1.0 + jnp.exp(jnp.minimum(z, SOFTPLUS_CLAMP))))


def _attn_kernel(q_ref, k_ref, v_ref, tri_ref, wrest_ref, wua_ref, wub_ref, wo_ref,
                 o_ref, wrest16_ref, wua16_ref, wub16_ref, wo16_ref, acc_ref, carry_ref, more_ref):
    for src, dst in ((wrest_ref, wrest16_ref), (wua_ref, wua16_ref), (wub_ref, wub16_ref),
                     (wo_ref, wo16_ref)):
        dst[...] = src[...].astype(BF16)

    t0 = pl.program_id(1) * QROWS
    lane = lax.broadcasted_iota(jnp.int32, (QT, LANES), 1)
    first_head = lane < HEAD_DIM
    row = lax.broadcasted_iota(jnp.int32, (2 * QT, BLK), 0) & (QT - 1)
    col = lax.broadcasted_iota(jnp.int32, (2 * QT, BLK), 1)

    def q_chain(p, r0):
        qt = q_ref[p, 0, pl.ds(r0, QT), :]
        zero = jnp.zeros_like(qt)
        return jnp.concatenate([jnp.where(first_head, qt, zero), jnp.where(first_head, zero, qt)],
                               axis=0)

    def scores(p, start, q2, visible, first_keys_visible=False):
        kb = k_ref[p, 0, pl.ds(start, BLK), :]
        z = lax.dot_general(q2, kb, (((1,), (1,)), ((), ())), preferred_element_type=F32)
        if first_keys_visible:
            return jnp.concatenate(
                [z[:, :QT], jnp.where(visible[:, QT:], z[:, QT:], MASKED)], axis=1)
        return jnp.where(visible, z, MASKED)

    def stick(z):
        csum = jnp.dot(_split_bf16(_softplus(z)), tri_ref[...], preferred_element_type=F32)
        return z - csum, csum[:, 0:1]

    def values(p, start, log_w, total, carry):
        if carry is not None:
            log_w = log_w - jnp.concatenate([carry] * (BLK // LANES), axis=1)
        vb = v_ref[p, 0, pl.ds(start, BLK), :]
        pv = jnp.dot(jnp.exp(log_w).astype(BF16), vb, preferred_element_type=F32)
        return pv, jnp.broadcast_to(total, (2 * QT, LANES))

    def window_start(tile):
        return jnp.maximum(t0 + (tile + 1) * QT - BLK, 0)

    chains = [(tile, p) for tile in range(TILES) for p in range(PAIRS)]
    starts = [pl.multiple_of(window_start(tile), QT) for tile in range(TILES)]
    zs, sticks, least = {}, {}, {}

    def stage_scores(c):
        tile, p = chains[c]
        visible = col - row < t0 + tile * QT - starts[tile]
        zs[c] = scores(p, starts[tile], q_chain(p, tile * QT), visible, tile >= 1)

    def stage_stick(c):
        sticks[c] = stick(zs[c])

    def stage_values(c):
        tile, p = chains[c]
        pv, tot = values(p, starts[tile], *sticks[c], None)
        acc_ref[c] = pv
        carry_ref[c] = tot
        least[c] = jnp.min(tot, axis=0, keepdims=True)

    stages = (stage_scores, stage_stick, stage_values)
    n_groups = CHAINS // GROUP
    for slot in range(n_groups + len(stages) - 1):
        for s, stage in enumerate(stages):
            g = slot - s
            if 0 <= g < n_groups:
                for c in range(g * GROUP, (g + 1) * GROUP):
                    stage(c)
    n_more = jnp.int32(0)
    for c, (tile, p) in enumerate(chains):
        more_ref[n_more] = jnp.int32(c)
        listed = jnp.logical_and(starts[tile] > 0, least[c][0, 0] < STICK_GONE)
        n_more = n_more + listed.astype(jnp.int32)

    def older_keys(j, _):
        c = more_ref[j]
        tile = c // PAIRS
        p = c % PAIRS
        q2 = q_chain(p, pl.multiple_of(tile * QT, QT))

        def body(state):
            end, _ = state
            start = pl.multiple_of(jnp.maximum(end - BLK, 0), QT)
            pv, tot = values(p, start, *stick(scores(p, start, q2, col < end - start)),
                             carry_ref[c])
            acc_ref[c] += pv
            carry = carry_ref[c] + tot
            carry_ref[c] = carry
            return start, jnp.logical_and(start > 0, jnp.min(carry) < STICK_GONE)

        lax.while_loop(lambda state: state[1], body, (window_start(tile), True))
        return 0

    lax.fori_loop(0, n_more, older_keys, 0)

    for tile in range(TILES):
        for p in range(PAIRS):
            a = acc_ref[tile * PAIRS + p]
            o_ref[0, tile * QT:(tile + 1) * QT, p * LANES:(p + 1) * LANES] = jnp.where(
                first_head, a[0:QT], a[QT:2 * QT])


def _attention(q, k, v, tri, w_in, w_up_a, w_up_b, w_out):
    _, b, s, _ = q.shape
    steps_per_batch = s // QROWS
    slab_in, slab_out, slab_shapes = _weight_slab_specs(
        b * steps_per_batch, lambda bi, i: bi * steps_per_batch + i)
    seq_spec = pl.BlockSpec((PAIRS, 1, s, LANES), lambda bi, i: (0, bi, 0, 0))
    return pl.pallas_call(
        _attn_kernel,
        grid=(b, s // QROWS),
        in_specs=[pl.BlockSpec((PAIRS, 1, QROWS, LANES), lambda bi, i: (0, bi, i, 0)),
                  seq_spec, seq_spec,
                  pl.BlockSpec((2 * BLK, BLK), lambda bi, i: (0, 0))] + slab_in,
        out_specs=[pl.BlockSpec((1, QROWS, D_SB), lambda bi, i: (bi, i, 0))] + slab_out,
        out_shape=[jax.ShapeDtypeStruct((b, s, D_SB), F32)] + slab_shapes,
        scratch_shapes=[pltpu.VMEM((CHAINS, 2 * QT, LANES), F32),
                        pltpu.VMEM((CHAINS, 2 * QT, LANES), F32),
                        pltpu.SMEM((CHAINS,), jnp.int32)],
        compiler_params=pltpu.CompilerParams(
            dimension_semantics=("arbitrary", "arbitrary"),
            vmem_limit_bytes=VMEM_LIMIT),
        name="stickbreak",
    )(q, k, v, tri, w_in, w_up_a, w_up_b, w_out)


def _tail_kernel(x_ref, o_ref, ng_ref, w_ref, lng_ref, lnb_ref, ws_ref, bs_ref, gm_ref,
                 wua_ref, wub_ref, wo_ref, fg_ref, out_ref, mixed_ref):
    tiles = [slice(r0, r0 + SUB_TAIL) for r0 in range(0, ROWS_TAIL, SUB_TAIL)]
    c_za, c_ub, c_vb, c_zb, c_ga, c_gb = 0, 512, 1024, 1536, 2048, 3072
    half = lax.broadcasted_iota(jnp.int32, (SUB_TAIL, LANES), 1) // GROUP_DIM
    pos_t = lax.broadcasted_iota(jnp.int32, (2 * SGU_CHUNK, SGU_CHUNK), 0) % SGU_CHUNK
    pos_s = lax.broadcasted_iota(jnp.int32, (2 * SGU_CHUNK, SGU_CHUNK), 1)
    causal = (pos_s // CHUNK) <= (pos_t // CHUNK)
    lane = lax.broadcasted_iota(jnp.int32, (SGU_CHUNK, LANES), 1)
    first_group = lane < GROUP_DIM

    def spread(stats):
        return jnp.concatenate(
            [jnp.take_along_axis(stats, 2 * p + half, axis=1) for p in range(PAIRS)], axis=1)

    state = [dict() for _ in tiles]

    def proj(s, lo, hi):
        return jnp.dot(s["h"], w_ref[:, lo:hi], preferred_element_type=F32)

    def norm_vb(s, t):
        s["x"] = x_ref[t, :]
        s["h"] = _rms_h(s["x"], ng_ref[...])
        s["vg"] = jax.nn.gelu(proj(s, c_vb, c_zb))

    def mean(s, t):
        mu = spread(jnp.dot(_split_bf16(s["vg"]), gm_ref[...], preferred_element_type=F32))
        s["d"] = s["vg"] - mu

    def ub(s, t):
        s["ug"] = jax.nn.gelu(proj(s, c_ub, c_vb))

    def variance(s, t):
        d = s["d"]
        var = spread(jnp.dot((d * d).astype(BF16), gm_ref[0:D_SGU, :], preferred_element_type=F32))
        s["vn"] = (d * lax.rsqrt(var + EPS) * lng_ref[...] + lnb_ref[...]).astype(BF16)

    def zb(s, t):
        s["sz_b"] = jax.nn.silu(proj(s, c_zb, c_ga))

    def za(s, t):
        s["sz_a"] = jax.nn.silu(proj(s, c_za, c_ub))

    def mix(s, t):
        vn = s["vn"]
        for p in range(PAIRS):
            wsp = jnp.where(causal, ws_ref[p], 0.0).astype(BF16)
            lanes = slice(p * LANES, (p + 1) * LANES)
            for c in range(0, SUB_TAIL, 2 * SGU_CHUNK):
                rows = [slice(c + j * SGU_CHUNK, c + (j + 1) * SGU_CHUNK) for j in range(2)]
                r = jnp.dot(wsp, jnp.concatenate([vn[rows[0], lanes], vn[rows[1], lanes]], axis=1),
                            preferred_element_type=F32)
                for j in range(2):
                    rj = r[:, j * LANES:(j + 1) * LANES]
                    mixed_ref[t.start + rows[j].start:t.start + rows[j].stop, lanes] = jnp.where(
                        first_group, rj[0:SGU_CHUNK], rj[SGU_CHUNK:2 * SGU_CHUNK])

    def gate_a(s, t):
        s["gate_a"] = jax.nn.sigmoid(proj(s, c_ga, c_gb))

    def up_a(s, t):
        p_a = jnp.dot((o_ref[t, :] * s["sz_a"]).astype(BF16), wua_ref[...],
                      preferred_element_type=F32)
        s["merged"] = s["gate_a"] * p_a

    def gate_b(s, t):
        s["gate_b"] = jax.nn.sigmoid(proj(s, c_gb, D_REST))

    def up_b(s, t):
        bias = jnp.concatenate([bs_ref[...]] * (SUB_TAIL // SGU_CHUNK), axis=0)
        p_b = jnp.dot((s["ug"] * (mixed_ref[t, :] + bias) * s["sz_b"]).astype(BF16),
                      wub_ref[...], preferred_element_type=F32)
        s["merged"] = s["merged"] + s["gate_b"] * p_b

    def out(s, t):
        y = s["x"] + jnp.dot(s["merged"].astype(BF16), wo_ref[...], preferred_element_type=F32)
        ms = jnp.mean(y * y, axis=-1, keepdims=True)
        out_ref[t, :] = y * lax.rsqrt(ms + EPS) * fg_ref[...]

    stages = (norm_vb, mean, ub, variance, gate_a, mix, zb, za, up_a, gate_b, up_b, out)
    for slot in range(len(stages) + (len(tiles) - 1) * TAIL_SKEW):
        for i, t in enumerate(tiles):
            k = slot - i * TAIL_SKEW
            if 0 <= k < len(stages):
                stages[k](state[i], t)


def _tail(x2, o2, norm_g, w_rest, ln_g, ln_b, ws, bs, gm2, w_up_a, w_up_b, w_out, final_g):
    n = x2.shape[0]
    const = lambda shape: pl.BlockSpec(shape, lambda i: (0,) * len(shape),
                                       pipeline_mode=pl.Buffered(1))
    return pl.pallas_call(
        _tail_kernel,
        grid=(n // ROWS_TAIL,),
        in_specs=[pl.BlockSpec((ROWS_TAIL, D_MODEL), lambda i: (i, 0)),
                  pl.BlockSpec((ROWS_TAIL, D_SB), lambda i: (i, 0)),
                  const((1, D_MODEL)),
                  const((D_MODEL, D_REST)),
                  const((1, D_SGU)), const((1, D_SGU)),
                  const((PAIRS, 2 * SGU_CHUNK, SGU_CHUNK)),
                  const((SGU_CHUNK, D_SGU)),
                  const((2 * D_SGU, LANES)),
                  const((D_SB, D_MODEL)), const((D_SGU, D_MODEL)), const((D_MODEL, D_MODEL)),
                  const((1, D_MODEL))],
        out_specs=pl.BlockSpec((ROWS_TAIL, D_MODEL), lambda i: (i, 0)),
        out_shape=jax.ShapeDtypeStruct((n, D_MODEL), F32),
        scratch_shapes=[pltpu.VMEM((ROWS_TAIL, D_SGU), F32)],
        compiler_params=pltpu.CompilerParams(
            dimension_semantics=("arbitrary",), vmem_limit_bytes=VMEM_LIMIT),
        name="tail",
    )(x2, o2, norm_g, w_rest, ln_g, ln_b, ws, bs, gm2, w_up_a, w_up_b, w_out, final_g)


def _layer(x, norm_g, w_in, ln_g, ln_b, w_s, b_s, w_up_a, w_up_b, w_out, final_g):
    b, s, _ = x.shape
    x2 = x.reshape(b * s, D_MODEL)
    q, k, v = _qkv_proj(x2, norm_g.reshape(1, D_MODEL), w_in)

    j = jnp.arange(BLK)
    tri = (j[:, None] >= j[None, :]).astype(BF16)
    tri2 = jnp.concatenate([tri, tri], axis=0)
    pair_major = lambda a: a.reshape(PAIRS, b, s, LANES)
    o, w_rest16, w_up_a16, w_up_b16, w_out16 = _attention(
        pair_major(q), pair_major(k), pair_major(v), tri2, w_in, w_up_a, w_up_b, w_out)

    gmean = (jnp.arange(D_SGU)[:, None] // GROUP_DIM == jnp.arange(LANES)[None, :] % N_GROUPS)
    gmean = (gmean.astype(F32) / GROUP_DIM).astype(BF16)
    gm2 = jnp.concatenate([gmean, gmean], axis=0)
    bias = jnp.repeat(b_s.T, GROUP_DIM, axis=1)
    return _tail(x2, o.reshape(b * s, D_SB), norm_g.reshape(1, D_MODEL), w_rest16,
                 ln_g.reshape(1, D_SGU), ln_b.reshape(1, D_SGU),
                 w_s.reshape(PAIRS, 2 * SGU_CHUNK, SGU_CHUNK), bias, gm2,
                 w_up_a16, w_up_b16, w_out16,
                 final_g.reshape(1, D_MODEL)).reshape(b, s, D_MODEL)


def kernel(x, norm_g, w_in, sgu_ln_g, sgu_ln_b, w_spatial, b_spatial, w_up_a, w_up_b, w_out,
           final_norm_g):
    assert norm_g.shape[0] == 1, "single-layer block"
    assert x.shape[1] % QROWS == 0 and (x.shape[0] * x.shape[1]) % ROWS_PROJ == 0
    return _layer(x, norm_g[0], w_in[0], sgu_ln_g[0], sgu_ln_b[0], w_spatial[0], b_spatial[0],
                  w_up_a[0], w_up_b[0], w_out[0], final_norm_g)
```

```python
import jax
import jax.numpy as jnp
from jax import lax
from jax.experimental import pallas as pl
from jax.experimental.pallas import tpu as pltpu

D_MODEL = 1024
N_HEADS = 8
HEAD_DIM = 64
D_SB = N_HEADS * HEAD_DIM
N_GROUPS = 8
GROUP_DIM = 64
D_SGU = N_GROUPS * GROUP_DIM
SGU_CHUNK = 128
CHUNK = 64
EPS = 1e-6
D_QKV = 3 * D_SB
D_REST = D_SB + 3 * D_SGU + 2 * D_MODEL

LANES = 128
BF16_SUBLANES = 16
PAIRS = D_SB // LANES

ROWS_PROJ = 2048
SUB_PROJ = 512
ROWS_TAIL = 1024
SUB_TAIL = 512
TAIL_SKEW = 2
BLK = 256
QROWS = 1024
QT = 128
TILES = QROWS // QT
CHAINS = TILES * PAIRS
GROUP = 8

VMEM_LIMIT = 56 * 1024 * 1024

F32 = jnp.float32
BF16 = jnp.bfloat16


def _rms_h(x, g):
    ms = jnp.mean(x * x, axis=-1, keepdims=True)
    return (x * lax.rsqrt(ms + EPS) * g).astype(BF16)


def _split_bf16(a):
    hi = a.astype(BF16)
    lo = (a - hi.astype(F32)).astype(BF16)
    return jnp.concatenate([hi, lo], axis=-1)


def _qkv_kernel(x_ref, g_ref, w_ref, q_ref, k_ref, v_ref, w16_ref):
    @pl.when(pl.program_id(0) == 0)
    def _():
        w16_ref[...] = w_ref[...].astype(BF16)

    scale = HEAD_DIM ** -0.5
    for r0 in range(0, ROWS_PROJ, SUB_PROJ):
        rows = slice(r0, r0 + SUB_PROJ)
        h = _rms_h(x_ref[rows, :], g_ref[...])
        for j, (ref, mul) in enumerate(((q_ref, scale), (k_ref, None), (v_ref, None))):
            y = jnp.dot(h, w16_ref[:, j * D_SB:(j + 1) * D_SB], preferred_element_type=F32)
            if mul is not None:
                y = y * mul
            y = y.astype(BF16)
            for p in range(PAIRS):
                ref[p, rows, :] = y[:, p * LANES:(p + 1) * LANES]


def _qkv_proj(x2, norm_g, w_in):
    n = x2.shape[0]
    out = jax.ShapeDtypeStruct((PAIRS, n, LANES), BF16)
    row_spec = pl.BlockSpec((ROWS_PROJ, D_MODEL), lambda i: (i, 0))
    out_spec = pl.BlockSpec((PAIRS, ROWS_PROJ, LANES), lambda i: (0, i, 0))
    return pl.pallas_call(
        _qkv_kernel,
        grid=(n // ROWS_PROJ,),
        in_specs=[row_spec,
                  pl.BlockSpec((1, D_MODEL), lambda i: (0, 0)),
                  pl.BlockSpec((D_MODEL, D_QKV), lambda i: (0, 0), pipeline_mode=pl.Buffered(1))],
        out_specs=[out_spec, out_spec, out_spec],
        out_shape=[out, out, out],
        scratch_shapes=[pltpu.VMEM((D_MODEL, D_QKV), BF16)],
        compiler_params=pltpu.CompilerParams(
            dimension_semantics=("arbitrary",), vmem_limit_bytes=VMEM_LIMIT),
        name="qkv_proj",
    )(x2, norm_g, w_in)


def _weight_slab_specs(steps, step_of):
    cols = D_REST // steps
    rows_up, rows_out = D_SB // steps, D_MODEL // steps
    assert cols % LANES == 0 and D_QKV % cols == 0 and rows_up % BF16_SUBLANES == 0, "slab tiling"
    rest_in = pl.BlockSpec((D_MODEL, cols), lambda *g: (0, D_QKV // cols + step_of(*g)))
    rest_out = pl.BlockSpec((D_MODEL, cols), lambda *g: (0, step_of(*g)))
    up = pl.BlockSpec((rows_up, D_MODEL), lambda *g: (step_of(*g), 0))
    out = pl.BlockSpec((rows_out, D_MODEL), lambda *g: (step_of(*g), 0))
    shapes = [jax.ShapeDtypeStruct((D_MODEL, D_REST), BF16),
              jax.ShapeDtypeStruct((D_SB, D_MODEL), BF16),
              jax.ShapeDtypeStruct((D_SGU, D_MODEL), BF16),
              jax.ShapeDtypeStruct((D_MODEL, D_MODEL), BF16)]
    return [rest_in, up, up, out], [rest_out, up, up, out], shapes


STICK_GONE = 88.0
MASKED = -1e30
SOFTPLUS_CLAMP = 40.0


def _softplus(z):
    return jnp.maximum(z, jnp.log(1.0 + jnp.exp(jnp.minimum(z, SOFTPLUS_CLAMP))))


def _attn_kernel(q_ref, k_ref, v_ref, tri_ref, wrest_ref, wua_ref, wub_ref, wo_ref,
                 o_ref, wrest16_ref, wua16_ref, wub16_ref, wo16_ref, acc_ref, carry_ref, more_ref):
    for src, dst in ((wrest_ref, wrest16_ref), (wua_ref, wua16_ref), (wub_ref, wub16_ref),
                     (wo_ref, wo16_ref)):
        dst[...] = src[...].astype(BF16)

    t0 = pl.program_id(1) * QROWS
    lane = lax.broadcasted_iota(jnp.int32, (QT, LANES), 1)
    first_head = lane < HEAD_DIM
    row = lax.broadcasted_iota(jnp.int32, (2 * QT, BLK), 0) & (QT - 1)
    col = lax.broadcasted_iota(jnp.int32, (2 * QT, BLK), 1)

    def q_chain(p, r0):
        qt = q_ref[p, 0, pl.ds(r0, QT), :]
        zero = jnp.zeros_like(qt)
        return jnp.concatenate([jnp.where(first_head, qt, zero), jnp.where(first_head, zero, qt)],
                               axis=0)

    def scores(p, start, q2, visible, first_keys_visible=False):
        kb = k_ref[p, 0, pl.ds(start, BLK), :]
        z = lax.dot_general(q2, kb, (((1,), (1,)), ((), ())), preferred_element_type=F32)
        if first_keys_visible:
            return jnp.concatenate(
                [z[:, :QT], jnp.where(visible[:, QT:], z[:, QT:], MASKED)], axis=1)
        return jnp.where(visible, z, MASKED)

    def stick(z):
        csum = jnp.dot(_split_bf16(_softplus(z)), tri_ref[...], preferred_element_type=F32)
        return z - csum, csum[:, 0:1]

    def values(p, start, log_w, total, carry):
        if carry is not None:
            log_w = log_w - jnp.concatenate([carry] * (BLK // LANES), axis=1)
        vb = v_ref[p, 0, pl.ds(start, BLK), :]
        pv = jnp.dot(jnp.exp(log_w).astype(BF16), vb, preferred_element_type=F32)
        return pv, jnp.broadcast_to(total, (2 * QT, LANES))

    def window_start(tile):
        return jnp.maximum(t0 + (tile + 1) * QT - BLK, 0)

    chains = [(tile, p) for tile in range(TILES) for p in range(PAIRS)]
    starts = [pl.multiple_of(window_start(tile), QT) for tile in range(TILES)]
    zs, sticks, least = {}, {}, {}

    def stage_scores(c):
        tile, p = chains[c]
        visible = col - row < t0 + tile * QT - starts[tile]
        zs[c] = scores(p, starts[tile], q_chain(p, tile * QT), visible, tile >= 1)

    def stage_stick(c):
        sticks[c] = stick(zs[c])

    def stage_values(c):
        tile, p = chains[c]
        pv, tot = values(p, starts[tile], *sticks[c], None)
        acc_ref[c] = pv
        carry_ref[c] = tot
        least[c] = jnp.min(tot, axis=0, keepdims=True)

    stages = (stage_scores, stage_stick, stage_values)
    n_groups = CHAINS // GROUP
    for slot in range(n_groups + len(stages) - 1):
        for s, stage in enumerate(stages):
            g = slot - s
            if 0 <= g < n_groups:
                for c in range(g * GROUP, (g + 1) * GROUP):
                    stage(c)
    n_more = jnp.int32(0)
    for c, (tile, p) in enumerate(chains):
        more_ref[n_more] = jnp.int32(c)
        listed = jnp.logical_and(starts[tile] > 0, least[c][0, 0] < STICK_GONE)
        n_more = n_more + listed.astype(jnp.int32)

    def older_keys(j, _):
        c = more_ref[j]
        tile = c // PAIRS
        p = c % PAIRS
        q2 = q_chain(p, pl.multiple_of(tile * QT, QT))

        def body(state):
            end, _ = state
            start = pl.multiple_of(jnp.maximum(end - BLK, 0), QT)
            pv, tot = values(p, start, *stick(scores(p, start, q2, col < end - start)),
                             carry_ref[c])
            acc_ref[c] += pv
            carry = carry_ref[c] + tot
            carry_ref[c] = carry
            return start, jnp.logical_and(start > 0, jnp.min(carry) < STICK_GONE)

        lax.while_loop(lambda state: state[1], body, (window_start(tile), True))
        return 0

    lax.fori_loop(0, n_more, older_keys, 0)

    for tile in range(TILES):
        for p in range(PAIRS):
            a = acc_ref[tile * PAIRS + p]
            o_ref[0, tile * QT:(tile + 1) * QT, p * LANES:(p + 1) * LANES] = jnp.where(
                first_head, a[0:QT], a[QT:2 * QT])


def _attention(q, k, v, tri, w_in, w_up_a, w_up_b, w_out):
    _, b, s, _ = q.shape
    steps_per_batch = s // QROWS
    slab_in, slab_out, slab_shapes = _weight_slab_specs(
        b * steps_per_batch, lambda bi, i: bi * steps_per_batch + i)
    seq_spec = pl.BlockSpec((PAIRS, 1, s, LANES), lambda bi, i: (0, bi, 0, 0))
    return pl.pallas_call(
        _attn_kernel,
        grid=(b, s // QROWS),
        in_specs=[pl.BlockSpec((PAIRS, 1, QROWS, LANES), lambda bi, i: (0, bi, i, 0)),
                  seq_spec, seq_spec,
                  pl.BlockSpec((2 * BLK, BLK), lambda bi, i: (0, 0))] + slab_in,
        out_specs=[pl.BlockSpec((1, QROWS, D_SB), lambda bi, i: (bi, i, 0))] + slab_out,
        out_shape=[jax.ShapeDtypeStruct((b, s, D_SB), F32)] + slab_shapes,
        scratch_shapes=[pltpu.VMEM((CHAINS, 2 * QT, LANES), F32),
                        pltpu.VMEM((CHAINS, 2 * QT, LANES), F32),
                        pltpu.SMEM((CHAINS,), jnp.int32)],
        compiler_params=pltpu.CompilerParams(
            dimension_semantics=("arbitrary", "arbitrary"),
            vmem_limit_bytes=VMEM_LIMIT),
        name="stickbreak",
    )(q, k, v, tri, w_in, w_up_a, w_up_b, w_out)


def _tail_kernel(x_ref, o_ref, ng_ref, w_ref, lng_ref, lnb_ref, ws_ref, bs_ref, gm_ref,
                 wua_ref, wub_ref, wo_ref, fg_ref, out_ref, mixed_ref):
    tiles = [slice(r0, r0 + SUB_TAIL) for r0 in range(0, ROWS_TAIL, SUB_TAIL)]
    c_za, c_ub, c_vb, c_zb, c_ga, c_gb = 0, 512, 1024, 1536, 2048, 3072
    half = lax.broadcasted_iota(jnp.int32, (SUB_TAIL, LANES), 1) // GROUP_DIM
    pos_t = lax.broadcasted_iota(jnp.int32, (2 * SGU_CHUNK, SGU_CHUNK), 0) % SGU_CHUNK
    pos_s = lax.broadcasted_iota(jnp.int32, (2 * SGU_CHUNK, SGU_CHUNK), 1)
    causal = (pos_s // CHUNK) <= (pos_t // CHUNK)
    lane = lax.broadcasted_iota(jnp.int32, (SGU_CHUNK, LANES), 1)
    first_group = lane < GROUP_DIM

    def spread(stats):
        return jnp.concatenate(
            [jnp.take_along_axis(stats, 2 * p + half, axis=1) for p in range(PAIRS)], axis=1)

    state = [dict() for _ in tiles]

    def proj(s, lo, hi):
        return jnp.dot(s["h"], w_ref[:, lo:hi], preferred_element_type=F32)

    def norm_vb(s, t):
        s["x"] = x_ref[t, :]
        s["h"] = _rms_h(s["x"], ng_ref[...])
        s["vg"] = jax.nn.gelu(proj(s, c_vb, c_zb))

    def mean(s, t):
        mu = spread(jnp.dot(_split_bf16(s["vg"]), gm_ref[...], preferred_element_type=F32))
        s["d"] = s["vg"] - mu

    def ub(s, t):
        s["ug"] = jax.nn.gelu(proj(s, c_ub, c_vb))

    def variance(s, t):
        d = s["d"]
        var = spread(jnp.dot((d * d).astype(BF16), gm_ref[0:D_SGU, :], preferred_element_type=F32))
        s["vn"] = (d * lax.rsqrt(var + EPS) * lng_ref[...] + lnb_ref[...]).astype(BF16)

    def zb(s, t):
        s["sz_b"] = jax.nn.silu(proj(s, c_zb, c_ga))

    def za(s, t):
        s["sz_a"] = jax.nn.silu(proj(s, c_za, c_ub))

    def mix(s, t):
        vn = s["vn"]
        for p in range(PAIRS):
            wsp = jnp.where(causal, ws_ref[p], 0.0).astype(BF16)
            lanes = slice(p * LANES, (p + 1) * LANES)
            for c in range(0, SUB_TAIL, 2 * SGU_CHUNK):
                rows = [slice(c + j * SGU_CHUNK, c + (j + 1) * SGU_CHUNK) for j in range(2)]
                r = jnp.dot(wsp, jnp.concatenate([vn[rows[0], lanes], vn[rows[1], lanes]], axis=1),
                            preferred_element_type=F32)
                for j in range(2):
                    rj = r[:, j * LANES:(j + 1) * LANES]
                    mixed_ref[t.start + rows[j].start:t.start + rows[j].stop, lanes] = jnp.where(
                        first_group, rj[0:SGU_CHUNK], rj[SGU_CHUNK:2 * SGU_CHUNK])

    def up_a(s, t):
        s["p_a"] = jnp.dot((o_ref[t, :] * s["sz_a"]).astype(BF16), wua_ref[...],
                           preferred_element_type=F32)

    def gate_a(s, t):
        s["merged"] = jax.nn.sigmoid(proj(s, c_ga, c_gb)) * s["p_a"]

    def up_b(s, t):
        bias = jnp.concatenate([bs_ref[...]] * (SUB_TAIL // SGU_CHUNK), axis=0)
        s["p_b"] = jnp.dot((s["ug"] * (mixed_ref[t, :] + bias) * s["sz_b"]).astype(BF16),
                           wub_ref[...], preferred_element_type=F32)

    def gate_b(s, t):
        s["merged"] = s["merged"] + jax.nn.sigmoid(proj(s, c_gb, D_REST)) * s["p_b"]

    def out(s, t):
        y = s["x"] + jnp.dot(s["merged"].astype(BF16), wo_ref[...], preferred_element_type=F32)
        ms = jnp.mean(y * y, axis=-1, keepdims=True)
        out_ref[t, :] = y * lax.rsqrt(ms + EPS) * fg_ref[...]

    stages = (norm_vb, mean, ub, variance, zb, za, mix, up_a, gate_a, up_b, gate_b, out)
    for slot in range(len(stages) + (len(tiles) - 1) * TAIL_SKEW):
        for i, t in enumerate(tiles):
            k = slot - i * TAIL_SKEW
            if 0 <= k < len(stages):
                stages[k](state[i], t)


def _tail(x2, o2, norm_g, w_rest, ln_g, ln_b, ws, bs, gm2, w_up_a, w_up_b, w_out, final_g):
    n = x2.shape[0]
    const = lambda shape: pl.BlockSpec(shape, lambda i: (0,) * len(shape),
                                       pipeline_mode=pl.Buffered(1))
    return pl.pallas_call(
        _tail_kernel,
        grid=(n // ROWS_TAIL,),
        in_specs=[pl.BlockSpec((ROWS_TAIL, D_MODEL), lambda i: (i, 0)),
                  pl.BlockSpec((ROWS_TAIL, D_SB), lambda i: (i, 0)),
                  const((1, D_MODEL)),
                  const((D_MODEL, D_REST)),
                  const((1, D_SGU)), const((1, D_SGU)),
                  const((PAIRS, 2 * SGU_CHUNK, SGU_CHUNK)),
                  const((SGU_CHUNK, D_SGU)),
                  const((2 * D_SGU, LANES)),
                  const((D_SB, D_MODEL)), const((D_SGU, D_MODEL)), const((D_MODEL, D_MODEL)),
                  const((1, D_MODEL))],
        out_specs=pl.BlockSpec((ROWS_TAIL, D_MODEL), lambda i: (i, 0)),
        out_shape=jax.ShapeDtypeStruct((n, D_MODEL), F32),
        scratch_shapes=[pltpu.VMEM((ROWS_TAIL, D_SGU), F32)],
        compiler_params=pltpu.CompilerParams(
            dimension_semantics=("arbitrary",), vmem_limit_bytes=VMEM_LIMIT),
        name="tail",
    )(x2, o2, norm_g, w_rest, ln_g, ln_b, ws, bs, gm2, w_up_a, w_up_b, w_out, final_g)


def _layer(x, norm_g, w_in, ln_g, ln_b, w_s, b_s, w_up_a, w_up_b, w_out, final_g):
    b, s, _ = x.shape
    x2 = x.reshape(b * s, D_MODEL)
    q, k, v = _qkv_proj(x2, norm_g.reshape(1, D_MODEL), w_in)

    j = jnp.arange(BLK)
    tri = (j[:, None] >= j[None, :]).astype(BF16)
    tri2 = jnp.concatenate([tri, tri], axis=0)
    pair_major = lambda a: a.reshape(PAIRS, b, s, LANES)
    o, w_rest16, w_up_a16, w_up_b16, w_out16 = _attention(
        pair_major(q), pair_major(k), pair_major(v), tri2, w_in, w_up_a, w_up_b, w_out)

    gmean = (jnp.arange(D_SGU)[:, None] // GROUP_DIM == jnp.arange(LANES)[None, :] % N_GROUPS)
    gmean = (gmean.astype(F32) / GROUP_DIM).astype(BF16)
    gm2 = jnp.concatenate([gmean, gmean], axis=0)
    bias = jnp.repeat(b_s.T, GROUP_DIM, axis=1)
    return _tail(x2, o.reshape(b * s, D_SB), norm_g.reshape(1, D_MODEL), w_rest16,
                 ln_g.reshape(1, D_SGU), ln_b.reshape(1, D_SGU),
                 w_s.reshape(PAIRS, 2 * SGU_CHUNK, SGU_CHUNK), bias, gm2,
                 w_up_a16, w_up_b16, w_out16,
                 final_g.reshape(1, D_MODEL)).reshape(b, s, D_MODEL)


def kernel(x, norm_g, w_in, sgu_ln_g, sgu_ln_b, w_spatial, b_spatial, w_up_a, w_up_b, w_out,
           final_norm_g):
    assert norm_g.shape[0] == 1, "single-layer block"
    assert x.shape[1] % QROWS == 0 and (x.shape[0] * x.shape[1]) % ROWS_PROJ == 0
    return _layer(x, norm_g[0], w_in[0], sgu_ln_g[0], sgu_ln_b[0], w_spatial[0], b_spatial[0],
                  w_up_a[0], w_up_b[0], w_out[0], final_norm_g)
```

```python
import jax
import jax.numpy as jnp
from jax import lax
from jax.experimental import pallas as pl
from jax.experimental.pallas import tpu as pltpu

D_MODEL = 1024
N_HEADS = 8
HEAD_DIM = 64
D_SB = N_HEADS * HEAD_DIM
N_GROUPS = 8
GROUP_DIM = 64
D_SGU = N_GROUPS * GROUP_DIM
SGU_CHUNK = 128
CHUNK = 64
EPS = 1e-6
D_QKV = 3 * D_SB
D_REST = D_SB + 3 * D_SGU + 2 * D_MODEL

LANES = 128
BF16_SUBLANES = 16
PAIRS = D_SB // LANES

ROWS_PROJ = 2048
SUB_PROJ = 512
ROWS_TAIL = 1024
SUB_TAIL = 512
TAIL_SKEW = 2
BLK = 256
QROWS = 1024
QT = 128
TILES = QROWS // QT
CHAINS = TILES * PAIRS
GROUP = 8

VMEM_LIMIT = 56 * 1024 * 1024

F32 = jnp.float32
BF16 = jnp.bfloat16


def _rms_h(x, g):
    ms = jnp.mean(x * x, axis=-1, keepdims=True)
    return (x * lax.rsqrt(ms + EPS) * g).astype(BF16)


def _split_bf16(a):
    hi = a.astype(BF16)
    lo = (a - hi.astype(F32)).astype(BF16)
    return jnp.concatenate([hi, lo], axis=-1)


def _qkv_kernel(x_ref, g_ref, w_ref, q_ref, k_ref, v_ref, w16_ref):
    @pl.when(pl.program_id(0) == 0)
    def _():
        w16_ref[...] = w_ref[...].astype(BF16)

    scale = HEAD_DIM ** -0.5
    for r0 in range(0, ROWS_PROJ, SUB_PROJ):
        rows = slice(r0, r0 + SUB_PROJ)
        h = _rms_h(x_ref[rows, :], g_ref[...])
        for j, (ref, mul) in enumerate(((q_ref, scale), (k_ref, None), (v_ref, None))):
            y = jnp.dot(h, w16_ref[:, j * D_SB:(j + 1) * D_SB], preferred_element_type=F32)
            if mul is not None:
                y = y * mul
            y = y.astype(BF16)
            for p in range(PAIRS):
                ref[p, rows, :] = y[:, p * LANES:(p + 1) * LANES]


def _qkv_proj(x2, norm_g, w_in):
    n = x2.shape[0]
    out = jax.ShapeDtypeStruct((PAIRS, n, LANES), BF16)
    row_spec = pl.BlockSpec((ROWS_PROJ, D_MODEL), lambda i: (i, 0))
    out_spec = pl.BlockSpec((PAIRS, ROWS_PROJ, LANES), lambda i: (0, i, 0))
    return pl.pallas_call(
        _qkv_kernel,
        grid=(n // ROWS_PROJ,),
        in_specs=[row_spec,
                  pl.BlockSpec((1, D_MODEL), lambda i: (0, 0)),
                  pl.BlockSpec((D_MODEL, D_QKV), lambda i: (0, 0), pipeline_mode=pl.Buffered(1))],
        out_specs=[out_spec, out_spec, out_spec],
        out_shape=[out, out, out],
        scratch_shapes=[pltpu.VMEM((D_MODEL, D_QKV), BF16)],
        compiler_params=pltpu.CompilerParams(
            dimension_semantics=("arbitrary",), vmem_limit_bytes=VMEM_LIMIT),
        name="qkv_proj",
    )(x2, norm_g, w_in)


def _weight_slab_specs(steps, step_of):
    cols = D_REST // steps
    rows_up, rows_out = D_SB // steps, D_MODEL // steps
    assert cols % LANES == 0 and D_QKV % cols == 0 and rows_up % BF16_SUBLANES == 0, "slab tiling"
    rest_in = pl.BlockSpec((D_MODEL, cols), lambda *g: (0, D_QKV // cols + step_of(*g)))
    rest_out = pl.BlockSpec((D_MODEL, cols), lambda *g: (0, step_of(*g)))
    up = pl.BlockSpec((rows_up, D_MODEL), lambda *g: (step_of(*g), 0))
    out = pl.BlockSpec((rows_out, D_MODEL), lambda *g: (step_of(*g), 0))
    shapes = [jax.ShapeDtypeStruct((D_MODEL, D_REST), BF16),
              jax.ShapeDtypeStruct((D_SB, D_MODEL), BF16),
              jax.ShapeDtypeStruct((D_SGU, D_MODEL), BF16),
              jax.ShapeDtypeStruct((D_MODEL, D_MODEL), BF16)]
    return [rest_in, up, up, out], [rest_out, up, up, out], shapes


STICK_GONE = 88.0
MASKED = -1e30
SOFTPLUS_CLAMP = 40.0


def _softplus(z):
    return jnp.maximum(z, jnp.log(1.0 + jnp.exp(jnp.minimum(z, SOFTPLUS_CLAMP))))


def _attn_kernel(q_ref, k_ref, v_ref, tri_ref, wrest_ref, wua_ref, wub_ref, wo_ref,
                 o_ref, wrest16_ref, wua16_ref, wub16_ref, wo16_ref, acc_ref, carry_ref, more_ref):
    for src, dst in ((wrest_ref, wrest16_ref), (wua_ref, wua16_ref), (wub_ref, wub16_ref),
                     (wo_ref, wo16_ref)):
        dst[...] = src[...].astype(BF16)

    t0 = pl.program_id(1) * QROWS
    lane = lax.broadcasted_iota(jnp.int32, (QT, LANES), 1)
    first_head = lane < HEAD_DIM
    row = lax.broadcasted_iota(jnp.int32, (2 * QT, BLK), 0) & (QT - 1)
    col = lax.broadcasted_iota(jnp.int32, (2 * QT, BLK), 1)

    def q_chain(p, r0):
        qt = q_ref[p, 0, pl.ds(r0, QT), :]
        zero = jnp.zeros_like(qt)
        return jnp.concatenate([jnp.where(first_head, qt, zero), jnp.where(first_head, zero, qt)],
                               axis=0)

    def scores(p, start, q2, visible, first_keys_visible=False):
        kb = k_ref[p, 0, pl.ds(start, BLK), :]
        z = lax.dot_general(q2, kb, (((1,), (1,)), ((), ())), preferred_element_type=F32)
        if first_keys_visible:
            return jnp.concatenate(
                [z[:, :QT], jnp.where(visible[:, QT:], z[:, QT:], MASKED)], axis=1)
        return jnp.where(visible, z, MASKED)

    def stick(z):
        csum = jnp.dot(_split_bf16(_softplus(z)), tri_ref[...], preferred_element_type=F32)
        return z - csum, csum[:, 0:1]

    def values(p, start, log_w, total, carry):
        if carry is not None:
            log_w = log_w - jnp.concatenate([carry] * (BLK // LANES), axis=1)
        vb = v_ref[p, 0, pl.ds(start, BLK), :]
        pv = jnp.dot(jnp.exp(log_w).astype(BF16), vb, preferred_element_type=F32)
        return pv, jnp.broadcast_to(total, (2 * QT, LANES))

    def window_start(tile):
        return jnp.maximum(t0 + (tile + 1) * QT - BLK, 0)

    chains = [(tile, p) for tile in range(TILES) for p in range(PAIRS)]
    starts = [pl.multiple_of(window_start(tile), QT) for tile in range(TILES)]
    zs, sticks, least = {}, {}, {}

    def stage_scores(c):
        tile, p = chains[c]
        visible = col - row < t0 + tile * QT - starts[tile]
        zs[c] = scores(p, starts[tile], q_chain(p, tile * QT), visible, tile >= 1)

    def stage_stick(c):
        sticks[c] = stick(zs[c])

    def stage_values(c):
        tile, p = chains[c]
        pv, tot = values(p, starts[tile], *sticks[c], None)
        acc_ref[c] = pv
        carry_ref[c] = tot
        least[c] = jnp.min(tot, axis=0, keepdims=True)

    stages = (stage_scores, stage_stick, stage_values)
    n_groups = CHAINS // GROUP
    for slot in range(n_groups + len(stages) - 1):
        for s, stage in enumerate(stages):
            g = slot - s
            if 0 <= g < n_groups:
                for c in range(g * GROUP, (g + 1) * GROUP):
                    stage(c)
    n_more = jnp.int32(0)
    for c, (tile, p) in enumerate(chains):
        more_ref[n_more] = jnp.int32(c)
        listed = jnp.logical_and(starts[tile] > 0, least[c][0, 0] < STICK_GONE)
        n_more = n_more + listed.astype(jnp.int32)

    def older_keys(j, _):
        c = more_ref[j]
        tile = c // PAIRS
        p = c % PAIRS
        q2 = q_chain(p, pl.multiple_of(tile * QT, QT))

        def body(state):
            end, _ = state
            start = pl.multiple_of(jnp.maximum(end - BLK, 0), QT)
            pv, tot = values(p, start, *stick(scores(p, start, q2, col < end - start)),
                             carry_ref[c])
            acc_ref[c] += pv
            carry = carry_ref[c] + tot
            carry_ref[c] = carry
            return start, jnp.logical_and(start > 0, jnp.min(carry) < STICK_GONE)

        lax.while_loop(lambda state: state[1], body, (window_start(tile), True))
        return 0

    lax.fori_loop(0, n_more, older_keys, 0)

    for tile in range(TILES):
        for p in range(PAIRS):
            a = acc_ref[tile * PAIRS + p]
            o_ref[0, tile * QT:(tile + 1) * QT, p * LANES:(p + 1) * LANES] = jnp.where(
                first_head, a[0:QT], a[QT:2 * QT])


def _attention(q, k, v, tri, w_in, w_up_a, w_up_b, w_out):
    _, b, s, _ = q.shape
    steps_per_batch = s // QROWS
    slab_in, slab_out, slab_shapes = _weight_slab_specs(
        b * steps_per_batch, lambda bi, i: bi * steps_per_batch + i)
    seq_spec = pl.BlockSpec((PAIRS, 1, s, LANES), lambda bi, i: (0, bi, 0, 0))
    return pl.pallas_call(
        _attn_kernel,
        grid=(b, s // QROWS),
        in_specs=[pl.BlockSpec((PAIRS, 1, QROWS, LANES), lambda bi, i: (0, bi, i, 0)),
                  seq_spec, seq_spec,
                  pl.BlockSpec((2 * BLK, BLK), lambda bi, i: (0, 0))] + slab_in,
        out_specs=[pl.BlockSpec((1, QROWS, D_SB), lambda bi, i: (bi, i, 0))] + slab_out,
        out_shape=[jax.ShapeDtypeStruct((b, s, D_SB), F32)] + slab_shapes,
        scratch_shapes=[pltpu.VMEM((CHAINS, 2 * QT, LANES), F32),
                        pltpu.VMEM((CHAINS, 2 * QT, LANES), F32),
                        pltpu.SMEM((CHAINS,), jnp.int32)],
        compiler_params=pltpu.CompilerParams(
            dimension_semantics=("arbitrary", "arbitrary"),
            vmem_limit_bytes=VMEM_LIMIT),
        name="stickbreak",
    )(q, k, v, tri, w_in, w_up_a, w_up_b, w_out)


def _tail_kernel(x_ref, o_ref, ng_ref, w_ref, lng_ref, lnb_ref, ws_ref, bs_ref, gm_ref,
                 wua_ref, wub_ref, wo_ref, fg_ref, out_ref, mixed_ref):
    tiles = [slice(r0, r0 + SUB_TAIL) for r0 in range(0, ROWS_TAIL, SUB_TAIL)]
    c_za, c_ub, c_vb, c_zb, c_ga, c_gb = 0, 512, 1024, 1536, 2048, 3072
    half = lax.broadcasted_iota(jnp.int32, (SUB_TAIL, LANES), 1) // GROUP_DIM
    pos_t = lax.broadcasted_iota(jnp.int32, (2 * SGU_CHUNK, SGU_CHUNK), 0) % SGU_CHUNK
    pos_s = lax.broadcasted_iota(jnp.int32, (2 * SGU_CHUNK, SGU_CHUNK), 1)
    causal = (pos_s // CHUNK) <= (pos_t // CHUNK)
    lane = lax.broadcasted_iota(jnp.int32, (SGU_CHUNK, LANES), 1)
    first_group = lane < GROUP_DIM

    def spread(stats):
        return jnp.concatenate(
            [jnp.take_along_axis(stats, 2 * p + half, axis=1) for p in range(PAIRS)], axis=1)

    state = [dict() for _ in tiles]

    def proj(s, lo, hi):
        return jnp.dot(s["h"], w_ref[:, lo:hi], preferred_element_type=F32)

    def norm_vb(s, t):
        s["x"] = x_ref[t, :]
        s["h"] = _rms_h(s["x"], ng_ref[...])
        s["vg"] = jax.nn.gelu(proj(s, c_vb, c_zb))

    def mean(s, t):
        mu = spread(jnp.dot(_split_bf16(s["vg"]), gm_ref[...], preferred_element_type=F32))
        s["d"] = s["vg"] - mu

    def ub(s, t):
        s["ug"] = jax.nn.gelu(proj(s, c_ub, c_vb))

    def variance(s, t):
        d = s["d"]
        var = spread(jnp.dot((d * d).astype(BF16), gm_ref[0:D_SGU, :], preferred_element_type=F32))
        s["vn"] = (d * lax.rsqrt(var + EPS) * lng_ref[...] + lnb_ref[...]).astype(BF16)

    def zb(s, t):
        s["sz_b"] = jax.nn.silu(proj(s, c_zb, c_ga))

    def za(s, t):
        s["sz_a"] = jax.nn.silu(proj(s, c_za, c_ub))

    def mix(s, t):
        vn = s["vn"]
        for p in range(PAIRS):
            wsp = jnp.where(causal, ws_ref[p], 0.0).astype(BF16)
            lanes = slice(p * LANES, (p + 1) * LANES)
            for c in range(0, SUB_TAIL, 2 * SGU_CHUNK):
                rows = [slice(c + j * SGU_CHUNK, c + (j + 1) * SGU_CHUNK) for j in range(2)]
                r = jnp.dot(wsp, jnp.concatenate([vn[rows[0], lanes], vn[rows[1], lanes]], axis=1),
                            preferred_element_type=F32)
                for j in range(2):
                    rj = r[:, j * LANES:(j + 1) * LANES]
                    mixed_ref[t.start + rows[j].start:t.start + rows[j].stop, lanes] = jnp.where(
                        first_group, rj[0:SGU_CHUNK], rj[SGU_CHUNK:2 * SGU_CHUNK])

    def up_a(s, t):
        s["p_a"] = jnp.dot((o_ref[t, :] * s["sz_a"]).astype(BF16), wua_ref[...],
                           preferred_element_type=F32)

    def gate_a(s, t):
        s["merged"] = jax.nn.sigmoid(proj(s, c_ga, c_gb)) * s["p_a"]

    def up_b(s, t):
        bias = jnp.concatenate([bs_ref[...]] * (SUB_TAIL // SGU_CHUNK), axis=0)
        s["p_b"] = jnp.dot((s["ug"] * (mixed_ref[t, :] + bias) * s["sz_b"]).astype(BF16),
                           wub_ref[...], preferred_element_type=F32)

    def gate_b(s, t):
        s["merged"] = s["merged"] + jax.nn.sigmoid(proj(s, c_gb, D_REST)) * s["p_b"]

    def out(s, t):
        y = s["x"] + jnp.dot(s["merged"].astype(BF16), wo_ref[...], preferred_element_type=F32)
        ms = jnp.mean(y * y, axis=-1, keepdims=True)
        out_ref[t, :] = y * lax.rsqrt(ms + EPS) * fg_ref[...]

    stages = (norm_vb, mean, ub, variance, zb, mix, za, up_a, gate_a, up_b, gate_b, out)
    for slot in range(len(stages) + (len(tiles) - 1) * TAIL_SKEW):
        for i, t in enumerate(tiles):
            k = slot - i * TAIL_SKEW
            if 0 <= k < len(stages):
                stages[k](state[i], t)


def _tail(x2, o2, norm_g, w_rest, ln_g, ln_b, ws, bs, gm2, w_up_a, w_up_b, w_out, final_g):
    n = x2.shape[0]
    const = lambda shape: pl.BlockSpec(shape, lambda i: (0,) * len(shape),
                                       pipeline_mode=pl.Buffered(1))
    return pl.pallas_call(
        _tail_kernel,
        grid=(n // ROWS_TAIL,),
        in_specs=[pl.BlockSpec((ROWS_TAIL, D_MODEL), lambda i: (i, 0)),
                  pl.BlockSpec((ROWS_TAIL, D_SB), lambda i: (i, 0)),
                  const((1, D_MODEL)),
                  const((D_MODEL, D_REST)),
                  const((1, D_SGU)), const((1, D_SGU)),
                  const((PAIRS, 2 * SGU_CHUNK, SGU_CHUNK)),
                  const((SGU_CHUNK, D_SGU)),
                  const((2 * D_SGU, LANES)),
                  const((D_SB, D_MODEL)), const((D_SGU, D_MODEL)), const((D_MODEL, D_MODEL)),
                  const((1, D_MODEL))],
        out_specs=pl.BlockSpec((ROWS_TAIL, D_MODEL), lambda i: (i, 0)),
        out_shape=jax.ShapeDtypeStruct((n, D_MODEL), F32),
        scratch_shapes=[pltpu.VMEM((ROWS_TAIL, D_SGU), F32)],
        compiler_params=pltpu.CompilerParams(
            dimension_semantics=("arbitrary",), vmem_limit_bytes=VMEM_LIMIT),
        name="tail",
    )(x2, o2, norm_g, w_rest, ln_g, ln_b, ws, bs, gm2, w_up_a, w_up_b, w_out, final_g)


def _layer(x, norm_g, w_in, ln_g, ln_b, w_s, b_s, w_up_a, w_up_b, w_out, final_g):
    b, s, _ = x.shape
    x2 = x.reshape(b * s, D_MODEL)
    q, k, v = _qkv_proj(x2, norm_g.reshape(1, D_MODEL), w_in)

    j = jnp.arange(BLK)
    tri = (j[:, None] >= j[None, :]).astype(BF16)
    tri2 = jnp.concatenate([tri, tri], axis=0)
    pair_major = lambda a: a.reshape(PAIRS, b, s, LANES)
    o, w_rest16, w_up_a16, w_up_b16, w_out16 = _attention(
        pair_major(q), pair_major(k), pair_major(v), tri2, w_in, w_up_a, w_up_b, w_out)

    gmean = (jnp.arange(D_SGU)[:, None] // GROUP_DIM == jnp.arange(LANES)[None, :] % N_GROUPS)
    gmean = (gmean.astype(F32) / GROUP_DIM).astype(BF16)
    gm2 = jnp.concatenate([gmean, gmean], axis=0)
    bias = jnp.repeat(b_s.T, GROUP_DIM, axis=1)
    return _tail(x2, o.reshape(b * s, D_SB), norm_g.reshape(1, D_MODEL), w_rest16,
                 ln_g.reshape(1, D_SGU), ln_b.reshape(1, D_SGU),
                 w_s.reshape(PAIRS, 2 * SGU_CHUNK, SGU_CHUNK), bias, gm2,
                 w_up_a16, w_up_b16, w_out16,
                 final_g.reshape(1, D_MODEL)).reshape(b, s, D_MODEL)


def kernel(x, norm_g, w_in, sgu_ln_g, sgu_ln_b, w_spatial, b_spatial, w_up_a, w_up_b, w_out,
           final_norm_g):
    assert norm_g.shape[0] == 1, "single-layer block"
    assert x.shape[1] % QROWS == 0 and (x.shape[0] * x.shape[1]) % ROWS_PROJ == 0
    return _layer(x, norm_g[0], w_in[0], sgu_ln_g[0], sgu_ln_b[0], w_spatial[0], b_spatial[0],
                  w_up_a[0], w_up_b[0], w_out[0], final_norm_g)
```

```python
import jax
import jax.numpy as jnp
from jax import lax
from jax.experimental import pallas as pl
from jax.experimental.pallas import tpu as pltpu

D_MODEL = 1024
N_HEADS = 8
HEAD_DIM = 64
D_SB = N_HEADS * HEAD_DIM
N_GROUPS = 8
GROUP_DIM = 64
D_SGU = N_GROUPS * GROUP_DIM
SGU_CHUNK = 128
CHUNK = 64
EPS = 1e-6
D_QKV = 3 * D_SB
D_REST = D_SB + 3 * D_SGU + 2 * D_MODEL

LANES = 128
BF16_SUBLANES = 16
PAIRS = D_SB // LANES

ROWS_PROJ = 2048
SUB_PROJ = 512
ROWS_TAIL = 1024
SUB_TAIL = 512
TAIL_SKEW = 2
BLK = 256
QROWS = 1024
QT = 128
TILES = QROWS // QT
CHAINS = TILES * PAIRS
GROUP = 8

VMEM_LIMIT = 56 * 1024 * 1024

F32 = jnp.float32
BF16 = jnp.bfloat16


def _rms_h(x, g):
    ms = jnp.mean(x * x, axis=-1, keepdims=True)
    return (x * lax.rsqrt(ms + EPS) * g).astype(BF16)


def _split_bf16(a):
    hi = a.astype(BF16)
    lo = (a - hi.astype(F32)).astype(BF16)
    return jnp.concatenate([hi, lo], axis=-1)


def _qkv_kernel(x_ref, g_ref, w_ref, q_ref, k_ref, v_ref, w16_ref):
    @pl.when(pl.program_id(0) == 0)
    def _():
        w16_ref[...] = w_ref[...].astype(BF16)

    scale = HEAD_DIM ** -0.5
    for r0 in range(0, ROWS_PROJ, SUB_PROJ):
        rows = slice(r0, r0 + SUB_PROJ)
        h = _rms_h(x_ref[rows, :], g_ref[...])
        for j, (ref, mul) in enumerate(((q_ref, scale), (k_ref, None), (v_ref, None))):
            y = jnp.dot(h, w16_ref[:, j * D_SB:(j + 1) * D_SB], preferred_element_type=F32)
            if mul is not None:
                y = y * mul
            y = y.astype(BF16)
            for p in range(PAIRS):
                ref[p, rows, :] = y[:, p * LANES:(p + 1) * LANES]


def _qkv_proj(x2, norm_g, w_in):
    n = x2.shape[0]
    out = jax.ShapeDtypeStruct((PAIRS, n, LANES), BF16)
    row_spec = pl.BlockSpec((ROWS_PROJ, D_MODEL), lambda i: (i, 0))
    out_spec = pl.BlockSpec((PAIRS, ROWS_PROJ, LANES), lambda i: (0, i, 0))
    return pl.pallas_call(
        _qkv_kernel,
        grid=(n // ROWS_PROJ,),
        in_specs=[row_spec,
                  pl.BlockSpec((1, D_MODEL), lambda i: (0, 0)),
                  pl.BlockSpec((D_MODEL, D_QKV), lambda i: (0, 0), pipeline_mode=pl.Buffered(1))],
        out_specs=[out_spec, out_spec, out_spec],
        out_shape=[out, out, out],
        scratch_shapes=[pltpu.VMEM((D_MODEL, D_QKV), BF16)],
        compiler_params=pltpu.CompilerParams(
            dimension_semantics=("arbitrary",), vmem_limit_bytes=VMEM_LIMIT),
        name="qkv_proj",
    )(x2, norm_g, w_in)


def _weight_slab_specs(steps, step_of):
    cols = D_REST // steps
    rows_up, rows_out = D_SB // steps, D_MODEL // steps
    assert cols % LANES == 0 and D_QKV % cols == 0 and rows_up % BF16_SUBLANES == 0, "slab tiling"
    rest_in = pl.BlockSpec((D_MODEL, cols), lambda *g: (0, D_QKV // cols + step_of(*g)))
    rest_out = pl.BlockSpec((D_MODEL, cols), lambda *g: (0, step_of(*g)))
    up = pl.BlockSpec((rows_up, D_MODEL), lambda *g: (step_of(*g), 0))
    out = pl.BlockSpec((rows_out, D_MODEL), lambda *g: (step_of(*g), 0))
    shapes = [jax.ShapeDtypeStruct((D_MODEL, D_REST), BF16),
              jax.ShapeDtypeStruct((D_SB, D_MODEL), BF16),
              jax.ShapeDtypeStruct((D_SGU, D_MODEL), BF16),
              jax.ShapeDtypeStruct((D_MODEL, D_MODEL), BF16)]
    return [rest_in, up, up, out], [rest_out, up, up, out], shapes


STICK_GONE = 88.0
MASKED = -1e30
SOFTPLUS_CLAMP = 40.0


def _softplus(z):
    return jnp.maximum(z, jnp.log(1.0 + jnp.exp(jnp.minimum(z, SOFTPLUS_CLAMP))))


def _attn_kernel(q_ref, k_ref, v_ref, tri_ref, wrest_ref, wua_ref, wub_ref, wo_ref,
                 o_ref, wrest16_ref, wua16_ref, wub16_ref, wo16_ref, acc_ref, carry_ref, more_ref):
    for src, dst in ((wrest_ref, wrest16_ref), (wua_ref, wua16_ref), (wub_ref, wub16_ref),
                     (wo_ref, wo16_ref)):
        dst[...] = src[...].astype(BF16)

    t0 = pl.program_id(1) * QROWS
    lane = lax.broadcasted_iota(jnp.int32, (QT, LANES), 1)
    first_head = lane < HEAD_DIM
    row = lax.broadcasted_iota(jnp.int32, (2 * QT, BLK), 0) & (QT - 1)
    col = lax.broadcasted_iota(jnp.int32, (2 * QT, BLK), 1)

    def q_chain(p, r0):
        qt = q_ref[p, 0, pl.ds(r0, QT), :]
        zero = jnp.zeros_like(qt)
        return jnp.concatenate([jnp.where(first_head, qt, zero), jnp.where(first_head, zero, qt)],
                               axis=0)

    def scores(p, start, q2, visible, first_keys_visible=False):
        kb = k_ref[p, 0, pl.ds(start, BLK), :]
        z = lax.dot_general(q2, kb, (((1,), (1,)), ((), ())), preferred_element_type=F32)
        if first_keys_visible:
            return jnp.concatenate(
                [z[:, :QT], jnp.where(visible[:, QT:], z[:, QT:], MASKED)], axis=1)
        return jnp.where(visible, z, MASKED)

    def stick(z):
        csum = jnp.dot(_split_bf16(_softplus(z)), tri_ref[...], preferred_element_type=F32)
        return z - csum, csum[:, 0:1]

    def values(p, start, log_w, total, carry):
        if carry is not None:
            log_w = log_w - jnp.concatenate([carry] * (BLK // LANES), axis=1)
        vb = v_ref[p, 0, pl.ds(start, BLK), :]
        pv_t = lax.dot_general(vb, jnp.exp(log_w).astype(BF16), (((0,), (1,)), ((), ())),
                               preferred_element_type=F32)
        return pv_t.T, jnp.broadcast_to(total, (2 * QT, LANES))

    def window_start(tile):
        return jnp.maximum(t0 + (tile + 1) * QT - BLK, 0)

    chains = [(tile, p) for tile in range(TILES) for p in range(PAIRS)]
    starts = [pl.multiple_of(window_start(tile), QT) for tile in range(TILES)]
    zs, sticks, least = {}, {}, {}

    def stage_scores(c):
        tile, p = chains[c]
        visible = col - row < t0 + tile * QT - starts[tile]
        zs[c] = scores(p, starts[tile], q_chain(p, tile * QT), visible, tile >= 1)

    def stage_stick(c):
        sticks[c] = stick(zs[c])

    def stage_values(c):
        tile, p = chains[c]
        pv, tot = values(p, starts[tile], *sticks[c], None)
        acc_ref[c] = pv
        carry_ref[c] = tot
        least[c] = jnp.min(tot, axis=0, keepdims=True)

    stages = (stage_scores, stage_stick, stage_values)
    n_groups = CHAINS // GROUP
    for slot in range(n_groups + len(stages) - 1):
        for s, stage in enumerate(stages):
            g = slot - s
            if 0 <= g < n_groups:
                for c in range(g * GROUP, (g + 1) * GROUP):
                    stage(c)
    n_more = jnp.int32(0)
    for c, (tile, p) in enumerate(chains):
        more_ref[n_more] = jnp.int32(c)
        listed = jnp.logical_and(starts[tile] > 0, least[c][0, 0] < STICK_GONE)
        n_more = n_more + listed.astype(jnp.int32)

    def older_keys(j, _):
        c = more_ref[j]
        tile = c // PAIRS
        p = c % PAIRS
        q2 = q_chain(p, pl.multiple_of(tile * QT, QT))

        def body(state):
            end, _ = state
            start = pl.multiple_of(jnp.maximum(end - BLK, 0), QT)
            pv, tot = values(p, start, *stick(scores(p, start, q2, col < end - start)),
                             carry_ref[c])
            acc_ref[c] += pv
            carry = carry_ref[c] + tot
            carry_ref[c] = carry
            return start, jnp.logical_and(start > 0, jnp.min(carry) < STICK_GONE)

        lax.while_loop(lambda state: state[1], body, (window_start(tile), True))
        return 0

    lax.fori_loop(0, n_more, older_keys, 0)

    for tile in range(TILES):
        for p in range(PAIRS):
            a = acc_ref[tile * PAIRS + p]
            o_ref[0, tile * QT:(tile + 1) * QT, p * LANES:(p + 1) * LANES] = jnp.where(
                first_head, a[0:QT], a[QT:2 * QT])


def _attention(q, k, v, tri, w_in, w_up_a, w_up_b, w_out):
    _, b, s, _ = q.shape
    steps_per_batch = s // QROWS
    slab_in, slab_out, slab_shapes = _weight_slab_specs(
        b * steps_per_batch, lambda bi, i: bi * steps_per_batch + i)
    seq_spec = pl.BlockSpec((PAIRS, 1, s, LANES), lambda bi, i: (0, bi, 0, 0))
    return pl.pallas_call(
        _attn_kernel,
        grid=(b, s // QROWS),
        in_specs=[pl.BlockSpec((PAIRS, 1, QROWS, LANES), lambda bi, i: (0, bi, i, 0)),
                  seq_spec, seq_spec,
                  pl.BlockSpec((2 * BLK, BLK), lambda bi, i: (0, 0))] + slab_in,
        out_specs=[pl.BlockSpec((1, QROWS, D_SB), lambda bi, i: (bi, i, 0))] + slab_out,
        out_shape=[jax.ShapeDtypeStruct((b, s, D_SB), F32)] + slab_shapes,
        scratch_shapes=[pltpu.VMEM((CHAINS, 2 * QT, LANES), F32),
                        pltpu.VMEM((CHAINS, 2 * QT, LANES), F32),
                        pltpu.SMEM((CHAINS,), jnp.int32)],
        compiler_params=pltpu.CompilerParams(
            dimension_semantics=("arbitrary", "arbitrary"),
            vmem_limit_bytes=VMEM_LIMIT),
        name="stickbreak",
    )(q, k, v, tri, w_in, w_up_a, w_up_b, w_out)


def _tail_kernel(x_ref, o_ref, ng_ref, w_ref, lng_ref, lnb_ref, ws_ref, bs_ref, gm_ref,
                 wua_ref, wub_ref, wo_ref, fg_ref, out_ref, mixed_ref):
    tiles = [slice(r0, r0 + SUB_TAIL) for r0 in range(0, ROWS_TAIL, SUB_TAIL)]
    c_za, c_ub, c_vb, c_zb, c_ga, c_gb = 0, 512, 1024, 1536, 2048, 3072
    half = lax.broadcasted_iota(jnp.int32, (SUB_TAIL, LANES), 1) // GROUP_DIM
    pos_t = lax.broadcasted_iota(jnp.int32, (2 * SGU_CHUNK, SGU_CHUNK), 0) % SGU_CHUNK
    pos_s = lax.broadcasted_iota(jnp.int32, (2 * SGU_CHUNK, SGU_CHUNK), 1)
    causal = (pos_s // CHUNK) <= (pos_t // CHUNK)
    lane = lax.broadcasted_iota(jnp.int32, (SGU_CHUNK, LANES), 1)
    first_group = lane < GROUP_DIM

    def spread(stats):
        return jnp.concatenate(
            [jnp.take_along_axis(stats, 2 * p + half, axis=1) for p in range(PAIRS)], axis=1)

    state = [dict() for _ in tiles]

    def proj(s, lo, hi):
        return jnp.dot(s["h"], w_ref[:, lo:hi], preferred_element_type=F32)

    def norm_vb(s, t):
        s["x"] = x_ref[t, :]
        s["h"] = _rms_h(s["x"], ng_ref[...])
        s["vg"] = jax.nn.gelu(proj(s, c_vb, c_zb))

    def mean(s, t):
        mu = spread(jnp.dot(_split_bf16(s["vg"]), gm_ref[...], preferred_element_type=F32))
        s["d"] = s["vg"] - mu

    def ub(s, t):
        s["ug"] = jax.nn.gelu(proj(s, c_ub, c_vb))

    def variance(s, t):
        d = s["d"]
        var = spread(jnp.dot((d * d).astype(BF16), gm_ref[0:D_SGU, :], preferred_element_type=F32))
        s["vn"] = (d * lax.rsqrt(var + EPS) * lng_ref[...] + lnb_ref[...]).astype(BF16)

    def zb(s, t):
        s["sz_b"] = jax.nn.silu(proj(s, c_zb, c_ga))

    def za(s, t):
        s["sz_a"] = jax.nn.silu(proj(s, c_za, c_ub))

    def mix(s, t):
        vn = s["vn"]
        for p in range(PAIRS):
            wsp = jnp.where(causal, ws_ref[p], 0.0).astype(BF16)
            lanes = slice(p * LANES, (p + 1) * LANES)
            for c in range(0, SUB_TAIL, 2 * SGU_CHUNK):
                rows = [slice(c + j * SGU_CHUNK, c + (j + 1) * SGU_CHUNK) for j in range(2)]
                r = jnp.dot(wsp, jnp.concatenate([vn[rows[0], lanes], vn[rows[1], lanes]], axis=1),
                            preferred_element_type=F32)
                for j in range(2):
                    rj = r[:, j * LANES:(j + 1) * LANES]
                    mixed_ref[t.start + rows[j].start:t.start + rows[j].stop, lanes] = jnp.where(
                        first_group, rj[0:SGU_CHUNK], rj[SGU_CHUNK:2 * SGU_CHUNK])

    def up_a(s, t):
        s["p_a"] = jnp.dot((o_ref[t, :] * s["sz_a"]).astype(BF16), wua_ref[...],
                           preferred_element_type=F32)

    def gate_a(s, t):
        s["merged"] = jax.nn.sigmoid(proj(s, c_ga, c_gb)) * s["p_a"]

    def up_b(s, t):
        bias = jnp.concatenate([bs_ref[...]] * (SUB_TAIL // SGU_CHUNK), axis=0)
        s["p_b"] = jnp.dot((s["ug"] * (mixed_ref[t, :] + bias) * s["sz_b"]).astype(BF16),
                           wub_ref[...], preferred_element_type=F32)

    def gate_b(s, t):
        s["merged"] = s["merged"] + jax.nn.sigmoid(proj(s, c_gb, D_REST)) * s["p_b"]

    def out(s, t):
        y = s["x"] + jnp.dot(s["merged"].astype(BF16), wo_ref[...], preferred_element_type=F32)
        ms = jnp.mean(y * y, axis=-1, keepdims=True)
        out_ref[t, :] = y * lax.rsqrt(ms + EPS) * fg_ref[...]

    stages = (norm_vb, mean, ub, variance, zb, za, mix, up_a, gate_a, up_b, gate_b, out)
    for slot in range(len(stages) + (len(tiles) - 1) * TAIL_SKEW):
        for i, t in enumerate(tiles):
            k = slot - i * TAIL_SKEW
            if 0 <= k < len(stages):
                stages[k](state[i], t)


def _tail(x2, o2, norm_g, w_rest, ln_g, ln_b, ws, bs, gm2, w_up_a, w_up_b, w_out, final_g):
    n = x2.shape[0]
    const = lambda shape: pl.BlockSpec(shape, lambda i: (0,) * len(shape),
                                       pipeline_mode=pl.Buffered(1))
    return pl.pallas_call(
        _tail_kernel,
        grid=(n // ROWS_TAIL,),
        in_specs=[pl.BlockSpec((ROWS_TAIL, D_MODEL), lambda i: (i, 0)),
                  pl.BlockSpec((ROWS_TAIL, D_SB), lambda i: (i, 0)),
                  const((1, D_MODEL)),
                  const((D_MODEL, D_REST)),
                  const((1, D_SGU)), const((1, D_SGU)),
                  const((PAIRS, 2 * SGU_CHUNK, SGU_CHUNK)),
                  const((SGU_CHUNK, D_SGU)),
                  const((2 * D_SGU, LANES)),
                  const((D_SB, D_MODEL)), const((D_SGU, D_MODEL)), const((D_MODEL, D_MODEL)),
                  const((1, D_MODEL))],
        out_specs=pl.BlockSpec((ROWS_TAIL, D_MODEL), lambda i: (i, 0)),
        out_shape=jax.ShapeDtypeStruct((n, D_MODEL), F32),
        scratch_shapes=[pltpu.VMEM((ROWS_TAIL, D_SGU), F32)],
        compiler_params=pltpu.CompilerParams(
            dimension_semantics=("arbitrary",), vmem_limit_bytes=VMEM_LIMIT),
        name="tail",
    )(x2, o2, norm_g, w_rest, ln_g, ln_b, ws, bs, gm2, w_up_a, w_up_b, w_out, final_g)


def _layer(x, norm_g, w_in, ln_g, ln_b, w_s, b_s, w_up_a, w_up_b, w_out, final_g):
    b, s, _ = x.shape
    x2 = x.reshape(b * s, D_MODEL)
    q, k, v = _qkv_proj(x2, norm_g.reshape(1, D_MODEL), w_in)

    j = jnp.arange(BLK)
    tri = (j[:, None] >= j[None, :]).astype(BF16)
    tri2 = jnp.concatenate([tri, tri], axis=0)
    pair_major = lambda a: a.reshape(PAIRS, b, s, LANES)
    o, w_rest16, w_up_a16, w_up_b16, w_out16 = _attention(
        pair_major(q), pair_major(k), pair_major(v), tri2, w_in, w_up_a, w_up_b, w_out)

    gmean = (jnp.arange(D_SGU)[:, None] // GROUP_DIM == jnp.arange(LANES)[None, :] % N_GROUPS)
    gmean = (gmean.astype(F32) / GROUP_DIM).astype(BF16)
    gm2 = jnp.concatenate([gmean, gmean], axis=0)
    bias = jnp.repeat(b_s.T, GROUP_DIM, axis=1)
    return _tail(x2, o.reshape(b * s, D_SB), norm_g.reshape(1, D_MODEL), w_rest16,
                 ln_g.reshape(1, D_SGU), ln_b.reshape(1, D_SGU),
                 w_s.reshape(PAIRS, 2 * SGU_CHUNK, SGU_CHUNK), bias, gm2,
                 w_up_a16, w_up_b16, w_out16,
                 final_g.reshape(1, D_MODEL)).reshape(b, s, D_MODEL)


def kernel(x, norm_g, w_in, sgu_ln_g, sgu_ln_b, w_spatial, b_spatial, w_up_a, w_up_b, w_out,
           final_norm_g):
    assert norm_g.shape[0] == 1, "single-layer block"
    assert x.shape[1] % QROWS == 0 and (x.shape[0] * x.shape[1]) % ROWS_PROJ == 0
    return _layer(x, norm_g[0], w_in[0], sgu_ln_g[0], sgu_ln_b[0], w_spatial[0], b_spatial[0],
                  w_up_a[0], w_up_b[0], w_out[0], final_norm_g)
```
